```python
import jax, jax.numpy as jnp
from jax import lax
import numpy as np

D_MODEL = 2048
BATCH = 2
SEQ = 16384
DEPTH = 2

N_META = 16
CHUNK = 64
MLP_HIDDEN = 4 * D_MODEL
NORM_EPS = 1e-6
N_BRANCH = 3
L2_EPS = 1e-24

H_WIDTH = D_MODEL // 2
H_HEAD_DIM = 128
H_HEADS = H_WIDTH // H_HEAD_DIM

M_WIDTH = D_MODEL // 2
M_HEAD_DIM = 64
M_HEADS = M_WIDTH // M_HEAD_DIM
M_GROUPS = 2
M_HEADS_PER_GROUP = M_HEADS // M_GROUPS
M_STATE = 128
M_CONV = 4
M_CONV_CH = M_WIDTH + 2 * M_GROUPS * M_STATE
M_DT_MIN = 1e-3
M_DT_MAX = 1e-1

R_WIDTH = D_MODEL // 2
R_HEAD_DIM = 64
R_HEADS = R_WIDTH // R_HEAD_DIM
R_DECAY_RANK = max(32, int(round(1.8 * D_MODEL ** 0.5 / 32)) * 32)
R_AAA_RANK = max(32, int(round(1.8 * D_MODEL ** 0.5 / 32)) * 32)
R_MV_RANK = max(32, int(round(1.3 * D_MODEL ** 0.5 / 32)) * 32)
R_GATE_RANK = max(32, int(round(0.6 * D_MODEL ** 0.8 / 32)) * 32)
R_GN_EPS = 64e-5

H_COLS = 4 * H_WIDTH
M_COLS = M_WIDTH + M_CONV_CH + M_HEADS
R_COLS = 3 * R_WIDTH + R_DECAY_RANK + R_AAA_RANK + R_GATE_RANK
GATE_COLS = N_BRANCH * D_MODEL
IN_COLS = H_COLS + M_COLS + R_COLS + GATE_COLS

kernel_name = 'hybrid_hgrn2_mamba2_rwkv7_gated_parallel'


def rmsnorm(x, w):
    xf = x.astype(jnp.float32)
    y = xf * lax.rsqrt(jnp.mean(xf * xf, -1, keepdims=True) + NORM_EPS)
    return (y * w.astype(jnp.float32)).astype(x.dtype)


def token_shift(p, mu):
    prev = jnp.pad(p, ((0, 0), (1, 0), (0, 0)))[:, :-1]
    return p + (prev - p) * mu


def causal_depthwise_conv(u, w, b):
    out = lax.conv_general_dilated(u, w[:, None, :], window_strides=(1,), padding=[(M_CONV - 1, 0)],
                                   dimension_numbers=('NWC', 'WIO', 'NWC'), feature_group_count=u.shape[-1])
    return out + b


def masked_exp(mask, logits):
    return jnp.where(mask, jnp.exp(jnp.where(mask, logits, 0.0)), 0.0)


def hgrn2_mixer(q_raw, f_raw, i_raw, g_raw, lb, norm_w, valid):
    b_, l_, _ = q_raw.shape
    nc = l_ // CHUNK
    f32 = jnp.float32
    q = jax.nn.silu(q_raw.astype(f32))
    fr = f_raw.astype(f32)
    logf = jnp.log(lb + (1.0 - lb) * jax.nn.sigmoid(fr))
    k = (1.0 - lb) * jax.nn.sigmoid(-fr) * valid
    v = i_raw.astype(f32)

    def chunks(t):
        return t.reshape(b_, nc, CHUNK, H_HEADS, H_HEAD_DIM).transpose(1, 0, 3, 2, 4)

    causal = jnp.tril(jnp.ones((CHUNK, CHUNK), bool))[:, :, None]

    def step(s, inp):
        qc, kc, vc, gc = inp
        gcum = jnp.cumsum(gc, axis=2)
        rel = masked_exp(causal, gcum[:, :, :, None, :] - gcum[:, :, None, :, :])
        attn = jnp.einsum('bhik,bhjk,bhijk->bhij', qc, kc, rel)
        o = jnp.einsum('bhij,bhjv->bhiv', attn, vc) + jnp.einsum('bhik,bhkv->bhiv', qc * jnp.exp(gcum), s)
        glast = gcum[:, :, -1:, :]
        s = jnp.exp(glast[:, :, 0, :, None]) * s + jnp.einsum('bhjk,bhjv->bhkv', kc * jnp.exp(glast - gcum), vc)
        return s, o

    s0 = jnp.zeros((b_, H_HEADS, H_HEAD_DIM, H_HEAD_DIM), f32)
    _, o = lax.scan(step, s0, (chunks(q), chunks(k), chunks(v), chunks(logf)))
    o = o.transpose(1, 0, 3, 2, 4).reshape(b_, l_, H_HEADS, H_HEAD_DIM)
    o = o * jax.nn.sigmoid(g_raw.astype(f32)).reshape(b_, l_, H_HEADS, H_HEAD_DIM)
    o = o * lax.rsqrt(jnp.mean(o * o, -1, keepdims=True) + NORM_EPS) * norm_w
    return o.reshape(b_, l_, H_WIDTH).astype(q_raw.dtype)


def ssd_chunked(x, dt, a, bmat, cmat):
    b_, l_, g_, hg_, p_ = x.shape
    nc = l_ // CHUNK
    xc = (x * dt[..., None]).reshape(b_, nc, CHUNK, g_, hg_, p_)
    acum = jnp.cumsum((dt * a).reshape(b_, nc, CHUNK, g_, hg_), axis=2)
    bc = bmat.reshape(b_, nc, CHUNK, g_, -1)
    cc = cmat.reshape(b_, nc, CHUNK, g_, -1)
    causal = jnp.tril(jnp.ones((CHUNK, CHUNK), bool))[:, :, None, None]
    seg = acum[:, :, :, None] - acum[:, :, None, :]
    lmat = masked_exp(causal, seg)
    scores = jnp.einsum('bzign,bzjgn->bzijg', cc, bc)
    y_diag = jnp.einsum('bzijg,bzijgh,bzjghp->bzighp', scores, lmat, xc)
    to_end = jnp.exp(acum[:, :, -1:] - acum)
    states = jnp.einsum('bzjgn,bzjgh,bzjghp->bzghpn', bc, to_end, xc)
    chunk_decay = jnp.exp(acum[:, :, -1])

    def carry_step(h, inp):
        st, dec = inp
        return h * dec[..., None, None] + st, h

    h0 = jnp.zeros((b_, g_, hg_, p_, bc.shape[-1]), xc.dtype)
    _, h_in = lax.scan(carry_step, h0, (jnp.moveaxis(states, 1, 0), jnp.moveaxis(chunk_decay, 1, 0)))
    h_in = jnp.moveaxis(h_in, 0, 1)
    y_off = jnp.einsum('bzign,bzghpn,bzigh->bzighp', cc, h_in, jnp.exp(acum))
    return (y_diag + y_off).reshape(b_, l_, g_, hg_, p_)


def mamba2_mixer(z, xbc, dt_raw, conv_w, conv_b, dt_bias, a_log, d_skip, norm_w, valid):
    b_, l_, _ = z.shape
    f32 = jnp.float32
    xbc = jax.nn.silu(causal_depthwise_conv(xbc, conv_w, conv_b).astype(f32))
    xs = xbc[..., :M_WIDTH].reshape(b_, l_, M_GROUPS, M_HEADS_PER_GROUP, M_HEAD_DIM)
    bmat = xbc[..., M_WIDTH:M_WIDTH + M_GROUPS * M_STATE].reshape(b_, l_, M_GROUPS, M_STATE)
    cmat = xbc[..., M_WIDTH + M_GROUPS * M_STATE:].reshape(b_, l_, M_GROUPS, M_STATE)
    dt = (jax.nn.softplus(dt_raw.astype(f32) + dt_bias) * valid).reshape(b_, l_, M_GROUPS, M_HEADS_PER_GROUP)
    a = -jnp.exp(a_log.astype(f32)).reshape(M_GROUPS, M_HEADS_PER_GROUP)
    y = ssd_chunked(xs, dt, a, bmat, cmat) + d_skip.reshape(M_GROUPS, M_HEADS_PER_GROUP)[..., None] * xs
    y = y.reshape(b_, l_, M_GROUPS, M_WIDTH // M_GROUPS) * jax.nn.silu(z.astype(f32)).reshape(b_, l_, M_GROUPS, M_WIDTH // M_GROUPS)
    y = y * lax.rsqrt(jnp.mean(y * y, -1, keepdims=True) + NORM_EPS)
    return (y.reshape(b_, l_, M_WIDTH) * norm_w).astype(z.dtype)


def rwkv7_mixer(r, decay, k, v, a, g, k_k, k_a, r_k, gn_w, gn_b, valid):
    b_, l_, _ = r.shape

    def heads(t):
        return t.reshape(b_, l_, R_HEADS, R_HEAD_DIM)

    kk = heads(k * k_k)
    kk = kk * lax.rsqrt(jnp.maximum(jnp.sum(kk * kk, -1, keepdims=True), L2_EPS))
    kh = heads(k * (1.0 + (a - 1.0) * k_a) * valid)
    rh, vh, ah = heads(r), heads(v), heads(a)

    def seq(t):
        return jnp.moveaxis(t, 1, 0)

    def step(s, inp):
        r_t, w_t, k_t, v_t, a_t, b_t = inp
        sa = jnp.einsum('bhvk,bhk->bhv', s, a_t)
        s = s * w_t[:, :, None, :] + sa[..., None] * b_t[:, :, None, :] + v_t[..., None] * k_t[:, :, None, :]
        return s, jnp.einsum('bhvk,bhk->bhv', s, r_t)

    s0 = jnp.zeros((b_, R_HEADS, R_HEAD_DIM, R_HEAD_DIM), jnp.float32)
    _, o = lax.scan(step, s0, (seq(rh), seq(heads(decay)), seq(kh), seq(vh), seq(-kk), seq(kk * ah)))
    o = jnp.moveaxis(o, 0, 1)
    mu = jnp.mean(o, -1, keepdims=True)
    var = jnp.mean(jnp.square(o - mu), -1, keepdims=True)
    o = ((o - mu) * lax.rsqrt(var + R_GN_EPS)).reshape(b_, l_, R_WIDTH) * gn_w + gn_b
    o = o + (jnp.sum(rh * kh * r_k, -1, keepdims=True) * vh).reshape(b_, l_, R_WIDTH)
    return o * g


def setup_inputs(seed: int = 0) -> dict:
    key = jax.random.key(seed)
    ks = iter(jax.random.split(key, 48))
    f32 = jnp.float32
    D = D_MODEL

    def nrm(shape, scale):
        return scale * jax.random.normal(next(ks), shape, f32)

    def uni(shape, lo, hi):
        return jax.random.uniform(next(ks), shape, f32, lo, hi)

    dt0 = jnp.exp(uni((DEPTH, M_HEADS), float(np.log(M_DT_MIN)), float(np.log(M_DT_MAX))))
    return {
        'x': nrm((BATCH, SEQ, D), 1.0),
        'meta': nrm((N_META, D), 1.0),
        'ln1_w': 1.0 + nrm((DEPTH, D), 0.02),
        'ln2_w': 1.0 + nrm((DEPTH, D), 0.02),
        'lnf_w': 1.0 + nrm((D,), 0.02),
        'w_in': nrm((DEPTH, D, IN_COLS), D ** -0.5),
        'w_in_vres': nrm((DEPTH - 1, D, R_MV_RANK), D ** -0.5),
        'hg_lb_logits': nrm((DEPTH, H_WIDTH), 0.5),
        'hg_norm_w': 1.0 + nrm((DEPTH, H_HEADS, H_HEAD_DIM), 0.02),
        'm_conv_w': nrm((DEPTH, M_CONV, M_CONV_CH), M_CONV ** -0.5),
        'm_conv_b': nrm((DEPTH, M_CONV_CH), 0.01),
        'm_dt_bias': dt0 + jnp.log(-jnp.expm1(-dt0)),
        'm_a_log': jnp.log(uni((DEPTH, M_HEADS), 1.0, 16.0)),
        'm_d': 1.0 + nrm((DEPTH, M_HEADS), 0.02),
        'm_norm_w': 1.0 + nrm((DEPTH, M_WIDTH), 0.02),
        'r_mu': uni((DEPTH, R_COLS), 0.0, 1.0),
        'r_mu_vres': uni((DEPTH - 1, R_MV_RANK), 0.0, 1.0),
        'r_w0': uni((DEPTH, R_WIDTH), -6.5, -1.0),
        'r_w2': nrm((DEPTH, R_DECAY_RANK, R_WIDTH), 0.1 * R_DECAY_RANK ** -0.5),
        'r_a0': nrm((DEPTH, R_WIDTH), 0.1),
        'r_a2': nrm((DEPTH, R_AAA_RANK, R_WIDTH), 0.1 * R_AAA_RANK ** -0.5),
        'r_v0': 1.0 + nrm((DEPTH - 1, R_WIDTH), 0.02),
        'r_v2': nrm((DEPTH - 1, R_MV_RANK, R_WIDTH), 0.1 * R_MV_RANK ** -0.5),
        'r_g2': nrm((DEPTH, R_GATE_RANK, R_WIDTH), R_GATE_RANK ** -0.5),
        'r_k_k': 0.85 + nrm((DEPTH, R_WIDTH), 0.02),
        'r_k_a': 1.0 + nrm((DEPTH, R_WIDTH), 0.02),
        'r_r_k': nrm((DEPTH, R_HEADS, R_HEAD_DIM), 0.1),
        'r_gn_w': 1.0 + nrm((DEPTH, R_WIDTH), 0.02),
        'r_gn_b': nrm((DEPTH, R_WIDTH), 0.01),
        'w_up_h': nrm((DEPTH, H_WIDTH, D), H_WIDTH ** -0.5),
        'w_up_m': nrm((DEPTH, M_WIDTH, D), M_WIDTH ** -0.5),
        'w_up_r': nrm((DEPTH, R_WIDTH, D), R_WIDTH ** -0.5),
        'w_out': nrm((DEPTH, D, D), D ** -0.5),
        'w_mlp_in': nrm((DEPTH, D, MLP_HIDDEN), D ** -0.5),
        'w_mlp_out': nrm((DEPTH, MLP_HIDDEN, D), MLP_HIDDEN ** -0.5),
    }


def reference(x, meta, ln1_w, ln2_w, lnf_w, w_in, w_in_vres, hg_lb_logits, hg_norm_w,
              m_conv_w, m_conv_b, m_dt_bias, m_a_log, m_d, m_norm_w,
              r_mu, r_mu_vres, r_w0, r_w2, r_a0, r_a2, r_v0, r_v2, r_g2, r_k_k, r_k_a, r_r_k, r_gn_w, r_gn_b,
              w_up_h, w_up_m, w_up_r, w_out, w_mlp_in, w_mlp_out):
    f32 = jnp.float32
    b_, _, d_ = x.shape
    n_pad = CHUNK - N_META
    h = jnp.concatenate([jnp.zeros((b_, n_pad, d_), x.dtype),
                         jnp.broadcast_to(meta.astype(x.dtype), (b_, N_META, d_)), x], axis=1)
    l_ = h.shape[1]
    valid = (jnp.arange(l_) >= n_pad).astype(f32)[None, :, None]

    lb_sm = jax.nn.softmax(hg_lb_logits.astype(f32), axis=0)
    lb_all = jnp.cumsum(lb_sm, axis=0) - lb_sm[0]

    v_first = None
    for l in range(DEPTH):
        u = rmsnorm(h, ln1_w[l])
        w_comb = w_in[l] if l == 0 else jnp.concatenate([w_in[l], w_in_vres[l - 1]], axis=1)
        p = u @ w_comb
        p_h, p_m, p_r, p_gate, p_vres = jnp.split(
            p, [H_COLS, H_COLS + M_COLS, H_COLS + M_COLS + R_COLS, IN_COLS], axis=-1)

        q_raw, f_raw, i_raw, g_raw = jnp.split(p_h, 4, axis=-1)
        o_h = hgrn2_mixer(q_raw, f_raw, i_raw, g_raw, lb_all[l], hg_norm_w[l], valid)

        z, xbc, dt_raw = jnp.split(p_m, [M_WIDTH, M_WIDTH + M_CONV_CH], axis=-1)
        o_m = mamba2_mixer(z, xbc, dt_raw, m_conv_w[l], m_conv_b[l], m_dt_bias[l], m_a_log[l],
                           m_d[l], m_norm_w[l], valid)

        pr = token_shift(p_r.astype(f32), r_mu[l])
        r, k, v, wl, al, gl = jnp.split(pr, [R_WIDTH, 2 * R_WIDTH, 3 * R_WIDTH, 3 * R_WIDTH + R_DECAY_RANK,
                                             3 * R_WIDTH + R_DECAY_RANK + R_AAA_RANK], axis=-1)
        w_log = -jax.nn.softplus(-(r_w0[l] + jnp.tanh(wl) @ r_w2[l])) - 0.5
        decay = jnp.exp(-jnp.exp(w_log))
        a = jax.nn.sigmoid(r_a0[l] + al @ r_a2[l])
        if l == 0:
            v_first = v
        else:
            vl = token_shift(p_vres.astype(f32), r_mu_vres[l - 1])
            v = v + (v_first - v) * jax.nn.sigmoid(r_v0[l - 1] + vl @ r_v2[l - 1])
        g = jax.nn.sigmoid(gl) @ r_g2[l]
        o_r = rwkv7_mixer(r, decay, k, v, a, g, r_k_k[l], r_k_a[l], r_r_k[l], r_gn_w[l], r_gn_b[l], valid)

        gates = jax.nn.sigmoid(p_gate.astype(f32)).reshape(b_, l_, N_BRANCH, d_)
        mixed = (gates[:, :, 0] * (o_h @ w_up_h[l]) + gates[:, :, 1] * (o_m @ w_up_m[l])
                 + gates[:, :, 2] * (o_r @ w_up_r[l]))
        h = h + (valid * (mixed @ w_out[l])).astype(h.dtype)

        u = rmsnorm(h, ln2_w[l])
        h = h + (valid * (jnp.square(jax.nn.relu(u @ w_mlp_in[l])) @ w_mlp_out[l])).astype(h.dtype)

    return rmsnorm(h, lnf_w)[:, CHUNK:]
```

```python
import functools

import jax
import jax.numpy as jnp
from jax import lax
from jax.experimental import pallas as pl
from jax.experimental.pallas import tpu as pltpu

F32 = jnp.float32
BF16 = jnp.bfloat16
HIGHEST = lax.Precision.HIGHEST

LANES = 128

D_MODEL = 2048
DEPTH = 2
N_META = 16
CHUNK = 64
N_PAD = CHUNK - N_META
MLP_HIDDEN = 4 * D_MODEL
NORM_EPS = 1e-6
N_BRANCH = 3
L2_EPS = 1e-24

H_WIDTH = D_MODEL // 2
H_HEAD_DIM = 128
H_HEADS = H_WIDTH // H_HEAD_DIM

M_WIDTH = D_MODEL // 2
M_HEAD_DIM = 64
M_HEADS = M_WIDTH // M_HEAD_DIM
M_GROUPS = 2
M_STATE = 128
M_CONV = 4
M_BC = M_GROUPS * M_STATE

R_WIDTH = D_MODEL // 2
R_HEAD_DIM = 64
R_HEADS = R_WIDTH // R_HEAD_DIM
R_DECAY_RANK = max(32, int(round(1.8 * D_MODEL ** 0.5 / 32)) * 32)
R_AAA_RANK = max(32, int(round(1.8 * D_MODEL ** 0.5 / 32)) * 32)
R_MV_RANK = max(32, int(round(1.3 * D_MODEL ** 0.5 / 32)) * 32)
R_GATE_RANK = max(32, int(round(0.6 * D_MODEL ** 0.8 / 32)) * 32)
R_GN_EPS = 64e-5
N_PAIRS = R_HEADS // 2

H_COLS = 4 * H_WIDTH
M_CONV_CH = M_WIDTH + 2 * M_BC
M_COLS = M_WIDTH + M_CONV_CH + M_HEADS
R_COLS = 3 * R_WIDTH + R_DECAY_RANK + R_AAA_RANK + R_GATE_RANK
GATE_COLS = N_BRANCH * D_MODEL
IN_COLS = H_COLS + M_COLS + R_COLS + GATE_COLS

C_H = 0
C_GATE = C_H + H_COLS
C_Z = C_GATE + GATE_COLS
C_X = C_Z + M_WIDTH
C_RKV = C_X + M_WIDTH
C_BC = C_RKV + 3 * R_WIDTH
C_G1 = C_BC + 2 * M_BC
C_WA = C_G1 + R_GATE_RANK
C_DT = C_WA + 2 * LANES
C_VRES = C_DT + LANES
P_COLS = C_VRES + LANES

ROW_ALIGN = 1280
SEQ_BLOCK = 256
VMEM_LIMIT = 56 * 1024 * 1024

NN = (((1,), (0,)), ((), ()))
NT = (((1,), (1,)), ((), ()))
TN = (((0,), (0,)), ((), ()))


def _dot(a, b, dims=NN):
    return lax.dot_general(a.astype(BF16), b.astype(BF16), dims, preferred_element_type=F32)


def _dot_hi(a, b, dims=NN):
    return lax.dot_general(a, b, dims, precision=HIGHEST, preferred_element_type=F32)


def _sigmoid(x):
    return 1.0 / (1.0 + jnp.exp(-x))


def _silu(x):
    return x * _sigmoid(x)


def _softplus(x):
    return jnp.maximum(x, 0.0) + jnp.log(1.0 + jnp.exp(-jnp.abs(x)))


def _iota(shape, dim):
    return lax.broadcasted_iota(jnp.int32, shape, dim)


def _tri(n):
    return (_iota((n, n), 0) >= _iota((n, n), 1)).astype(F32)


def _block_diag(x):
    lane = _iota(x.shape, 1)
    return jnp.concatenate([jnp.where(lane < 64, x, 0.0), jnp.where(lane >= 64, x, 0.0)], axis=0)


def _params(sem):
    return pltpu.CompilerParams(dimension_semantics=sem, vmem_limit_bytes=VMEM_LIMIT)


def _proj_kernel(h_ref, lnw_ref, w_ref, o_ref, u_ref):
    @pl.when(pl.program_id(1) == 0)
    def _():
        x = h_ref[...]
        ms = jnp.mean(x * x, axis=-1, keepdims=True)
        u_ref[...] = (x * lax.rsqrt(ms + NORM_EPS) * lnw_ref[...]).astype(BF16)

    o_ref[...] = jnp.dot(u_ref[...], w_ref[...], preferred_element_type=F32)


def _proj(h, lnw, w, tm, tn):
    t = h.shape[0]
    return pl.pallas_call(
        _proj_kernel,
        grid=(t // tm, P_COLS // tn),
        in_specs=[
            pl.BlockSpec((tm, D_MODEL), lambda i, j: (i, 0)),
            pl.BlockSpec((1, D_MODEL), lambda i, j: (0, 0)),
            pl.BlockSpec((D_MODEL, tn), lambda i, j: (0, j)),
        ],
        out_specs=pl.BlockSpec((tm, tn), lambda i, j: (i, j)),
        out_shape=jax.ShapeDtypeStruct((t, P_COLS), F32),
        scratch_shapes=[pltpu.VMEM((tm, D_MODEL), BF16)],
        compiler_params=_params(("parallel", "arbitrary")),
        name="in_proj",
    )(h, lnw, w)


HG_SUB = 16


def _hgrn_kernel(q_ref, f_ref, i_ref, g_ref, lbl_ref, nw_ref, o_ref,
                 st_ref, qs, ks, gs, vs, os_, *, layer, rb):
    jb = pl.program_id(1)

    @pl.when(jb == 0)
    def _():
        st_ref[...] = jnp.zeros(st_ref.shape, F32)

    lg = lbl_ref[...]
    mx = jnp.max(lg, axis=0, keepdims=True)
    ex = jnp.exp(lg - mx)
    sm = ex / jnp.sum(ex, axis=0, keepdims=True)
    lb = jnp.sum(sm[0:layer + 1], axis=0, keepdims=True) - sm[0:1]

    tri = _tri(CHUNK)
    row = _iota((CHUNK, 1), 0)
    row_s = _iota((HG_SUB, 1), 0)
    arow = _iota((CHUNK, CHUNK), 0)
    acol = _iota((CHUNK, CHUNK), 1)
    off_mask = acol < (arow // HG_SUB) * HG_SUB
    n_sub = CHUNK // HG_SUB

    def chunk(c, carry):
        r0 = pl.multiple_of(c * CHUNK, CHUNK)
        rows = pl.ds(r0, CHUNK)
        valid = ((jb * rb + r0 + row) >= N_PAD).astype(F32)
        q = _silu(q_ref[rows, :])
        fr = f_ref[rows, :]
        logf = jnp.log(lb + (1.0 - lb) * _sigmoid(fr))
        k = (1.0 - lb) * _sigmoid(-fr) * valid
        v = i_ref[rows, :]
        gc = _dot_hi(tri, logf)
        qs[...] = q
        ks[...] = k
        gs[...] = gc
        vs[...] = v

        for sub in range(n_sub):
            base = sub * HG_SUB
            g_i = gs[base:base + HG_SUB, :]
            q_i = qs[base:base + HG_SUB, :]

            def jbody(jj, acc, base=base, g_i=g_i, q_i=q_i):
                g_j = gs[pl.ds(base + jj, 1), :]
                k_j = ks[pl.ds(base + jj, 1), :]
                v_j = vs[pl.ds(base + jj, 1), :]
                e = jnp.exp(jnp.minimum(g_i - g_j, 0.0)) * q_i * k_j
                parts = []
                for hd in range(H_HEADS):
                    hs = slice(hd * H_HEAD_DIM, (hd + 1) * H_HEAD_DIM)
                    s = jnp.sum(e[:, hs], axis=-1, keepdims=True)
                    parts.append(s * v_j[:, hs])
                contrib = jnp.concatenate(parts, axis=-1)
                return acc + jnp.where(row_s >= jj, contrib, 0.0)

            os_[base:base + HG_SUB, :] = lax.fori_loop(
                0, HG_SUB, jbody, jnp.zeros((HG_SUB, H_WIDTH), F32))

        g_last = gc[CHUNK - 1:CHUNK, :]
        for hd in range(H_HEADS):
            hs = slice(hd * H_HEAD_DIM, (hd + 1) * H_HEAD_DIM)
            g_h, q_h, k_h, v_h = gc[:, hs], q[:, hs], k[:, hs], v[:, hs]
            qcat, kcat = [], []
            for sub in range(1, n_sub):
                base = sub * HG_SUB
                g_r = g_h[base:base + 1, :]
                in_sub = (row >= base) & (row < base + HG_SUB)
                qcat.append(jnp.where(in_sub, q_h * jnp.exp(jnp.minimum(g_h - g_r, 0.0)), 0.0))
                kcat.append(k_h * jnp.exp(jnp.minimum(g_r - g_h, 0.0)))
            attn = _dot(jnp.concatenate(qcat, axis=1), jnp.concatenate(kcat, axis=1), NT)
            attn = jnp.where(off_mask, attn, 0.0)
            st = st_ref[hd]
            o_h = _dot(attn, v_h) + _dot(q_h * jnp.exp(g_h), st, NT)
            os_[:, hs] = os_[:, hs] + o_h
            gl_h = g_last[:, hs]
            st_ref[hd] = st * jnp.exp(gl_h) + _dot(v_h, k_h * jnp.exp(gl_h - g_h), TN)

        o = os_[...] * _sigmoid(g_ref[rows, :])
        outs = []
        for hd in range(H_HEADS):
            hs = slice(hd * H_HEAD_DIM, (hd + 1) * H_HEAD_DIM)
            o_h = o[:, hs]
            outs.append(o_h * lax.rsqrt(jnp.mean(o_h * o_h, axis=-1, keepdims=True) + NORM_EPS))
        o_ref[rows, :] = (jnp.concatenate(outs, axis=-1) * nw_ref[...]).astype(o_ref.dtype)
        return carry

    lax.fori_loop(0, rb // CHUNK, chunk, 0)


def _hgrn(p, lb_logits, norm_w, layer, batch, lp):
    rb = SEQ_BLOCK
    nb = lp // rb
    t = p.shape[0]
    cb = C_H // H_WIDTH

    def col(k):
        return pl.BlockSpec((rb, H_WIDTH), lambda b, j, k=k: (b * nb + j, cb + k))

    return pl.pallas_call(
        functools.partial(_hgrn_kernel, layer=layer, rb=rb),
        grid=(batch, nb),
        in_specs=[col(0), col(1), col(2), col(3),
                  pl.BlockSpec((DEPTH, H_WIDTH), lambda b, j: (0, 0)),
                  pl.BlockSpec((1, H_WIDTH), lambda b, j: (0, 0))],
        out_specs=pl.BlockSpec((rb, H_WIDTH), lambda b, j: (b * nb + j, 0)),
        out_shape=jax.ShapeDtypeStruct((t, H_WIDTH), BF16),
        scratch_shapes=[pltpu.VMEM((H_HEADS, H_HEAD_DIM, H_HEAD_DIM), F32)]
        + [pltpu.VMEM((CHUNK, H_WIDTH), F32) for _ in range(5)],
        compiler_params=_params(("parallel", "arbitrary")),
        name="hgrn2",
    )(p, p, p, p, lb_logits, norm_w)


HIST = 8


def _mamba_kernel(z_ref, x_ref, bc_ref, dt_ref, cwx_ref, cwb_ref, cbx_ref, cbb_ref,
                  dtb_ref, alog_ref, dsk_ref, nw_ref, o_ref,
                  st_ref, xe, be, xa, ba, *, rb):
    jb = pl.program_id(1)

    @pl.when(jb == 0)
    def _():
        st_ref[...] = jnp.zeros(st_ref.shape, F32)
        xe[0:HIST, :] = jnp.zeros((HIST, M_WIDTH), F32)
        be[0:HIST, :] = jnp.zeros((HIST, 2 * M_BC), F32)

    def conv(ext, src_ref, w_ref, b_ref, dst):
        ext[HIST:HIST + rb, :] = src_ref[...]
        acc = b_ref[...] + w_ref[0:1, :] * ext[HIST - 3:HIST - 3 + rb, :]
        for tap in range(1, M_CONV):
            o = HIST - (M_CONV - 1) + tap
            acc = acc + w_ref[tap:tap + 1, :] * ext[o:o + rb, :]
        dst[...] = _silu(acc)
        ext[0:HIST, :] = ext[rb:rb + HIST, :]

    conv(xe, x_ref, cwx_ref, cbx_ref, xa)
    conv(be, bc_ref, cwb_ref, cbb_ref, ba)

    tri = _tri(CHUNK)
    ones = jnp.ones((CHUNK, CHUNK), F32)
    row = _iota((CHUNK, 1), 0)
    expand = (_iota((LANES, M_WIDTH), 1) // M_HEAD_DIM == _iota((LANES, M_WIDTH), 0)).astype(F32)
    pos_j = _iota((CHUNK, M_WIDTH), 1) % CHUNK
    row_i = _iota((CHUNK, M_WIDTH), 0)
    a_neg = -jnp.exp(alog_ref[...])
    hpg = M_HEADS // M_GROUPS

    def chunk(c, carry):
        r0 = pl.multiple_of(c * CHUNK, CHUNK)
        rows = pl.ds(r0, CHUNK)
        valid = ((jb * rb + r0 + row) >= N_PAD).astype(F32)
        xs = xa[rows, :]
        bcm = ba[rows, :]
        dt = _softplus(dt_ref[rows, :] + dtb_ref[...]) * valid
        dt_e = _dot_hi(dt, expand)
        da_e = _dot_hi(dt * a_neg, expand)
        acum = _dot_hi(tri, da_e)
        acum_j = _dot_hi(ones, jnp.where(row_i <= pos_j, da_e, 0.0))
        lmat = jnp.where(row_i >= pos_j, jnp.exp(jnp.minimum(acum - acum_j, 0.0)), 0.0)
        a_last = acum[CHUNK - 1:CHUNK, :]
        e_cum = jnp.exp(acum)
        e_end = jnp.exp(a_last - acum)
        e_last = jnp.exp(a_last)
        xc = xs * dt_e
        scores = []
        for g in range(M_GROUPS):
            b_g = bcm[:, g * M_STATE:(g + 1) * M_STATE]
            c_g = bcm[:, M_BC + g * M_STATE:M_BC + (g + 1) * M_STATE]
            scores.append(_dot(c_g, jnp.concatenate([b_g] * hpg, axis=0), NT))
        attn = jnp.concatenate(scores, axis=1) * lmat
        ys = []
        for pr in range(M_HEADS // 2):
            ps = slice(pr * LANES, (pr + 1) * LANES)
            g = (2 * pr) // hpg
            b_g = bcm[:, g * M_STATE:(g + 1) * M_STATE]
            c_g = bcm[:, M_BC + g * M_STATE:M_BC + (g + 1) * M_STATE]
            xc_p = xc[:, ps]
            st = st_ref[pr]
            y = _dot(attn[:, ps], _block_diag(xc_p)) + _dot(c_g, st) * e_cum[:, ps]
            st_ref[pr] = st * e_last[:, ps] + _dot(b_g, xc_p * e_end[:, ps], TN)
            ys.append(y)
        y = (jnp.concatenate(ys, axis=1) + dsk_ref[...] * xs) * _silu(z_ref[rows, :])
        gw = M_WIDTH // M_GROUPS
        outs = []
        for g in range(M_GROUPS):
            yg = y[:, g * gw:(g + 1) * gw]
            outs.append(yg * lax.rsqrt(jnp.mean(yg * yg, axis=-1, keepdims=True) + NORM_EPS))
        o_ref[rows, :] = (jnp.concatenate(outs, axis=1) * nw_ref[...]).astype(o_ref.dtype)
        return carry

    lax.fori_loop(0, rb // CHUNK, chunk, 0)


def _mamba(p, conv_w, conv_b, dt_bias, a_log, d_skip, norm_w, batch, lp):
    rb = SEQ_BLOCK
    nb = lp // rb
    t = p.shape[0]

    def col(off, width):
        return pl.BlockSpec((rb, width), lambda b, j: (b * nb + j, off // width))

    def full(shape):
        return pl.BlockSpec(shape, lambda b, j: (0,) * len(shape))

    pad = LANES - M_HEADS
    dtb = jnp.pad(dt_bias.reshape(1, M_HEADS), ((0, 0), (0, pad)))
    alog = jnp.pad(a_log.reshape(1, M_HEADS), ((0, 0), (0, pad)))
    dsk = jnp.repeat(d_skip, M_HEAD_DIM).reshape(1, M_WIDTH)
    return pl.pallas_call(
        functools.partial(_mamba_kernel, rb=rb),
        grid=(batch, nb),
        in_specs=[col(C_Z, M_WIDTH), col(C_X, M_WIDTH), col(C_BC, 2 * M_BC), col(C_DT, LANES),
                  full((M_CONV, M_WIDTH)), full((M_CONV, 2 * M_BC)),
                  full((1, M_WIDTH)), full((1, 2 * M_BC)),
                  full((1, LANES)), full((1, LANES)), full((1, M_WIDTH)), full((1, M_WIDTH))],
        out_specs=pl.BlockSpec((rb, M_WIDTH), lambda b, j: (b * nb + j, 0)),
        out_shape=jax.ShapeDtypeStruct((t, M_WIDTH), BF16),
        scratch_shapes=[pltpu.VMEM((M_HEADS // 2, M_STATE, LANES), F32),
                        pltpu.VMEM((rb + HIST, M_WIDTH), F32),
                        pltpu.VMEM((rb + HIST, 2 * M_BC), F32),
                        pltpu.VMEM((rb, M_WIDTH), F32),
                        pltpu.VMEM((rb, 2 * M_BC), F32)],
        compiler_params=_params(("parallel", "arbitrary")),
        name="mamba2",
    )(p, p, p, p, conv_w[:, :M_WIDTH], conv_w[:, M_WIDTH:], conv_b[None, :M_WIDTH],
      conv_b[None, M_WIDTH:], dtb, alog, dsk, norm_w[None, :])


R_SHIFT_W = 3 * R_WIDTH


def _rwkv_kernel(*refs, first, rb):
    if first:
        (rkv_ref, g1_ref, wa_ref, mu_rkv, mu_g1, mu_wa,
         w0_ref, w2_ref, a0_ref, a2_ref, g2_ref, kk_ref, ka_ref, rk_ref, gnw_ref, gnb_ref,
         o_ref, vf_out,
         st_ref, e_rkv, e_g1, e_wa, lw_s, r_s, k_s, v_s, a_s, b_s, o_s, g_s) = refs
    else:
        (rkv_ref, g1_ref, wa_ref, vr_ref, vf_ref, mu_rkv, mu_g1, mu_wa, mu_vr,
         w0_ref, w2_ref, a0_ref, a2_ref, g2_ref, kk_ref, ka_ref, rk_ref, gnw_ref, gnb_ref,
         v0_ref, v2_ref,
         o_ref,
         st_ref, e_rkv, e_g1, e_wa, e_vr, lw_s, r_s, k_s, v_s, a_s, b_s, o_s, g_s) = refs
    jb = pl.program_id(1)
    exts = [e_rkv, e_g1, e_wa] + ([] if first else [e_vr])

    @pl.when(jb == 0)
    def _():
        st_ref[...] = jnp.zeros(st_ref.shape, F32)
        for ext in exts:
            ext[0:HIST, :] = jnp.zeros((HIST, ext.shape[1]), F32)

    def shift(ext, src_ref, mu_ref):
        ext[HIST:HIST + rb, :] = src_ref[...]
        cur = ext[HIST:HIST + rb, :]
        prev = ext[HIST - 1:HIST - 1 + rb, :]
        out = cur + (prev - cur) * mu_ref[...]
        ext[0:HIST, :] = ext[rb:rb + HIST, :]
        return out

    rkv = shift(e_rkv, rkv_ref, mu_rkv)
    gl = shift(e_g1, g1_ref, mu_g1)
    wa = shift(e_wa, wa_ref, mu_wa)
    r = rkv[:, 0:R_WIDTH]
    k = rkv[:, R_WIDTH:2 * R_WIDTH]
    v = rkv[:, 2 * R_WIDTH:3 * R_WIDTH]
    wl = wa[:, 0:LANES]
    al = wa[:, LANES:2 * LANES]

    valid = ((jb * rb + _iota((rb, 1), 0)) >= N_PAD).astype(F32)
    w_log = -_softplus(-(w0_ref[...] + _dot(jnp.tanh(wl), w2_ref[...]))) - 0.5
    lw_s[...] = -jnp.exp(w_log)
    a = _sigmoid(a0_ref[...] + _dot(al, a2_ref[...]))
    if first:
        vf_out[...] = v
    else:
        vl = shift(e_vr, vr_ref, mu_vr)
        v = v + (vf_ref[...] - v) * _sigmoid(v0_ref[...] + _dot(vl, v2_ref[...]))
    g_s[...] = _dot(_sigmoid(gl), g2_ref[...])

    seg_ones = (_iota((LANES, LANES), 0) // R_HEAD_DIM == _iota((LANES, LANES), 1) // R_HEAD_DIM).astype(F32)

    def head_sum(x):
        return jnp.concatenate(
            [_dot_hi(x[:, s * LANES:(s + 1) * LANES], seg_ones) for s in range(N_PAIRS)], axis=1)

    kk = k * kk_ref[...]
    kk = kk * lax.rsqrt(jnp.maximum(head_sum(kk * kk), L2_EPS))
    kh = k * (1.0 + (a - 1.0) * ka_ref[...]) * valid
    r_s[...] = r
    k_s[...] = kh
    v_s[...] = v
    a_s[...] = -kk
    b_s[...] = kk * a

    tri = _tri(CHUNK)
    t_i = _iota((CHUNK, LANES), 0)
    s_j = _iota((CHUNK, LANES), 1) % CHUNK
    strict = t_i > s_j
    incl = t_i >= s_j
    eye = (t_i == s_j).astype(F32)
    same_head = _iota((LANES, LANES), 0) // R_HEAD_DIM == _iota((LANES, LANES), 1) // R_HEAD_DIM
    n_double = 5
    assert 2 ** (n_double + 1) == CHUNK

    def chunk(c, carry):
        r0 = pl.multiple_of(c * CHUNK, CHUNK)
        rows = pl.ds(r0, CHUNK)
        for pr in range(N_PAIRS):
            ps = slice(pr * LANES, (pr + 1) * LANES)
            lw = lw_s[rows, ps]
            r_c, k_c, v_c, a_c, b_c = r_s[rows, ps], k_s[rows, ps], v_s[rows, ps], a_s[rows, ps], b_s[rows, ps]
            cum = _dot_hi(tri, lw)
            c_last = cum[CHUNK - 1:CHUNK, :]
            e_inv = jnp.exp(-cum)
            e_end = jnp.exp(c_last - cum)
            a_t = a_c * jnp.exp(cum - lw)
            r_t = r_c * jnp.exp(cum)
            b_t = b_c * e_inv
            k_t = k_c * e_inv
            m = _dot(jnp.concatenate([a_t, r_t], axis=0),
                     jnp.concatenate([_block_diag(b_t), _block_diag(k_t)], axis=0), NT)
            a_ab = jnp.where(strict, m[0:CHUNK, 0:LANES], 0.0)
            a_ak = jnp.where(strict, m[0:CHUNK, LANES:2 * LANES], 0.0)
            a_rb = jnp.where(incl, m[CHUNK:2 * CHUNK, 0:LANES], 0.0)
            a_rk = jnp.where(incl, m[CHUNK:2 * CHUNK, LANES:2 * LANES], 0.0)
            inv = eye + a_ab
            pw = _dot(a_ab, _block_diag(a_ab))
            for it in range(n_double - 1):
                both = _dot(jnp.concatenate([pw, inv], axis=0), _block_diag(pw))
                pw = both[0:CHUNK]
                inv = inv + both[CHUNK:2 * CHUNK]
            inv = inv + _dot(inv, _block_diag(pw))
            st = st_ref[pr]
            vb = _block_diag(v_c)
            u = _dot(inv, _block_diag(_dot(a_t, st, NT) + _dot(a_ak, vb)))
            o_s[rows, ps] = _dot(r_t, st, NT) + _dot(a_rb, _block_diag(u)) + _dot(a_rk, vb)
            upd = _dot(jnp.concatenate([u, v_c], axis=0),
                       jnp.concatenate([b_c * e_end, k_c * e_end], axis=0), TN)
            st_ref[pr] = st * jnp.exp(c_last) + jnp.where(same_head, upd, 0.0)
        return carry

    lax.fori_loop(0, rb // CHUNK, chunk, 0)

    o = o_s[...]
    inv_n = 1.0 / R_HEAD_DIM
    mu = head_sum(o) * inv_n
    d = o - mu
    var = head_sum(d * d) * inv_n
    o = d * lax.rsqrt(var + R_GN_EPS) * gnw_ref[...] + gnb_ref[...]
    o = o + head_sum(r_s[...] * k_s[...] * rk_ref[...]) * v_s[...]
    o_ref[...] = (o * g_s[...]).astype(o_ref.dtype)


def _rwkv(p, v_first, prm, first, batch, lp):
    rb = SEQ_BLOCK
    nb = lp // rb
    t = p.shape[0]

    def col(off, width):
        return pl.BlockSpec((rb, width), lambda b, j: (b * nb + j, off // width))

    def full(a):
        return pl.BlockSpec(a.shape, lambda b, j: (0,) * a.ndim)

    row_spec = pl.BlockSpec((rb, R_WIDTH), lambda b, j: (b * nb + j, 0))
    acts = [p, p, p]
    act_specs = [col(C_RKV, R_SHIFT_W), col(C_G1, R_GATE_RANK), col(C_WA, 2 * LANES)]
    mus = [prm["mu_rkv"], prm["mu_g1"], prm["mu_wa"]]
    tail = []
    if not first:
        acts += [p, v_first]
        act_specs += [col(C_VRES, LANES), row_spec]
        mus.append(prm["mu_vr"])
        tail = [prm["v0"], prm["v2"]]
    consts = mus + [prm[n] for n in ("w0", "w2", "a0", "a2", "g2", "k_k", "k_a", "r_k", "gn_w", "gn_b")] + tail
    out_shape = [jax.ShapeDtypeStruct((t, R_WIDTH), BF16)]
    out_specs = [row_spec]
    if first:
        out_shape.append(jax.ShapeDtypeStruct((t, R_WIDTH), F32))
        out_specs.append(row_spec)
    ext_w = [R_SHIFT_W, R_GATE_RANK, 2 * LANES] + ([] if first else [LANES])
    scratch = ([pltpu.VMEM((N_PAIRS, LANES, LANES), F32)]
               + [pltpu.VMEM((rb + HIST, w), F32) for w in ext_w]
               + [pltpu.VMEM((rb, R_WIDTH), F32) for _ in range(8)])
    return pl.pallas_call(
        functools.partial(_rwkv_kernel, first=first, rb=rb),
        grid=(batch, nb),
        in_specs=act_specs + [full(a) for a in consts],
        out_specs=out_specs,
        out_shape=out_shape,
        scratch_shapes=scratch,
        compiler_params=_params(("parallel", "arbitrary")),
        name="rwkv7",
    )(*acts, *consts)


def _merge_kernel(oh_ref, om_ref, or_ref, g0_ref, g1_ref, g2_ref, wh_ref, wm_ref, wr_ref, o_ref):
    acc = _sigmoid(g0_ref[...]) * jnp.dot(oh_ref[...], wh_ref[...], preferred_element_type=F32)
    acc = acc + _sigmoid(g1_ref[...]) * jnp.dot(om_ref[...], wm_ref[...], preferred_element_type=F32)
    acc = acc + _sigmoid(g2_ref[...]) * jnp.dot(or_ref[...], wr_ref[...], preferred_element_type=F32)
    o_ref[...] = acc.astype(o_ref.dtype)


def _merge(o_h, o_m, o_r, p, w_h, w_m, w_r, tm, tn):
    t = p.shape[0]
    gb = C_GATE // tn
    nt = D_MODEL // tn

    def act():
        return pl.BlockSpec((tm, H_WIDTH), lambda i, j: (i, 0))

    def gate(k):
        return pl.BlockSpec((tm, tn), lambda i, j, k=k: (i, gb + k * nt + j))

    def wt():
        return pl.BlockSpec((H_WIDTH, tn), lambda i, j: (0, j))

    return pl.pallas_call(
        _merge_kernel,
        grid=(t // tm, nt),
        in_specs=[act(), act(), act(), gate(0), gate(1), gate(2), wt(), wt(), wt()],
        out_specs=pl.BlockSpec((tm, tn), lambda i, j: (i, j)),
        out_shape=jax.ShapeDtypeStruct((t, D_MODEL), BF16),
        compiler_params=_params(("parallel", "arbitrary")),
        name="merge",
    )(o_h, o_m, o_r, p, p, p, w_h, w_m, w_r)


def _row_valid(tm, lp):
    pos = (pl.program_id(0) * tm + _iota((tm, 1), 0)) % lp
    return (pos >= N_PAD).astype(F32)


def _outproj_kernel(h_ref, m_ref, w_ref, o_ref, *, tm, lp):
    upd = jnp.dot(m_ref[...], w_ref[...], preferred_element_type=F32)
    o_ref[...] = h_ref[...] + _row_valid(tm, lp) * upd


def _outproj(h, mixed, w_out, tm, lp):
    t = h.shape[0]
    return pl.pallas_call(
        functools.partial(_outproj_kernel, tm=tm, lp=lp),
        grid=(t // tm,),
        in_specs=[pl.BlockSpec((tm, D_MODEL), lambda i: (i, 0)),
                  pl.BlockSpec((tm, D_MODEL), lambda i: (i, 0)),
                  pl.BlockSpec((D_MODEL, D_MODEL), lambda i: (0, 0))],
        out_specs=pl.BlockSpec((tm, D_MODEL), lambda i: (i, 0)),
        out_shape=jax.ShapeDtypeStruct((t, D_MODEL), F32),
        compiler_params=_params(("parallel",)),
        name="out_proj",
    )(h, mixed, w_out)


def _mlp_kernel(h_ref, lnw_ref, w1_ref, w2_ref, lnf_ref, o_ref, u_ref, acc_ref, *, tm, lp, final):
    kk = pl.program_id(1)

    @pl.when(kk == 0)
    def _():
        x = h_ref[...]
        ms = jnp.mean(x * x, axis=-1, keepdims=True)
        u_ref[...] = (x * lax.rsqrt(ms + NORM_EPS) * lnw_ref[...]).astype(BF16)
        acc_ref[...] = jnp.zeros(acc_ref.shape, F32)

    a = jnp.maximum(jnp.dot(u_ref[...], w1_ref[...], preferred_element_type=F32), 0.0)
    acc_ref[...] += jnp.dot((a * a).astype(BF16), w2_ref[...], preferred_element_type=F32)

    @pl.when(kk == pl.num_programs(1) - 1)
    def _():
        y = h_ref[...] + _row_valid(tm, lp) * acc_ref[...]
        if final:
            ms = jnp.mean(y * y, axis=-1, keepdims=True)
            y = y * lax.rsqrt(ms + NORM_EPS) * lnf_ref[...]
        o_ref[...] = y


def _mlp(h, lnw, w1, w2, lnf, tm, th, lp, final):
    t = h.shape[0]
    return pl.pallas_call(
        functools.partial(_mlp_kernel, tm=tm, lp=lp, final=final),
        grid=(t // tm, MLP_HIDDEN // th),
        in_specs=[pl.BlockSpec((tm, D_MODEL), lambda i, k: (i, 0)),
                  pl.BlockSpec((1, D_MODEL), lambda i, k: (0, 0)),
                  pl.BlockSpec((D_MODEL, th), lambda i, k: (0, k)),
                  pl.BlockSpec((th, D_MODEL), lambda i, k: (k, 0)),
                  pl.BlockSpec((1, D_MODEL), lambda i, k: (0, 0))],
        out_specs=pl.BlockSpec((tm, D_MODEL), lambda i, k: (i, 0)),
        out_shape=jax.ShapeDtypeStruct((t, D_MODEL), F32),
        scratch_shapes=[pltpu.VMEM((tm, D_MODEL), BF16), pltpu.VMEM((tm, D_MODEL), F32)],
        compiler_params=_params(("parallel", "arbitrary")),
        name="mlp",
    )(h, lnw, w1, w2, lnf)


def _pad_cols(a, width):
    return jnp.pad(a, ((0, 0), (0, width - a.shape[1])))


def _pad_rows(a, height):
    return jnp.pad(a, ((0, height - a.shape[0]), (0, 0)))


def _split_in_cols(a, a_vres):
    o = 0
    hcols = a[:, o:o + H_COLS]; o += H_COLS
    z = a[:, o:o + M_WIDTH]; o += M_WIDTH
    xm = a[:, o:o + M_WIDTH]; o += M_WIDTH
    bc = a[:, o:o + 2 * M_BC]; o += 2 * M_BC
    dt = a[:, o:o + M_HEADS]; o += M_HEADS
    rkv = a[:, o:o + 3 * R_WIDTH]; o += 3 * R_WIDTH
    w1 = a[:, o:o + R_DECAY_RANK]; o += R_DECAY_RANK
    a1 = a[:, o:o + R_AAA_RANK]; o += R_AAA_RANK
    g1 = a[:, o:o + R_GATE_RANK]; o += R_GATE_RANK
    gates = a[:, o:o + GATE_COLS]; o += GATE_COLS
    assert o == IN_COLS
    out = jnp.concatenate([hcols, gates, z, xm, rkv, bc, g1, _pad_cols(w1, LANES), _pad_cols(a1, LANES),
                           _pad_cols(dt, LANES), _pad_cols(a_vres, LANES)], axis=1)
    assert out.shape[1] == P_COLS
    return out


def _tile(t, candidates):
    for c in candidates:
        if t % c == 0:
            return c
    raise ValueError(f"no tile for {t}")


def kernel(x, meta, ln1_w, ln2_w, lnf_w, w_in, w_in_vres, hg_lb_logits, hg_norm_w, m_conv_w, m_conv_b, m_dt_bias, m_a_log, m_d, m_norm_w, r_mu, r_mu_vres, r_w0, r_w2, r_a0, r_a2, r_v0, r_v2, r_g2, r_k_k, r_k_a, r_r_k, r_gn_w, r_gn_b, w_up_h, w_up_m, w_up_r, w_out, w_mlp_in, w_mlp_out):
    batch, seq, d = x.shape
    assert d == D_MODEL and M_HEAD_DIM == CHUNK and R_HEAD_DIM == CHUNK
    l_real = seq + CHUNK
    lp = -(-l_real // ROW_ALIGN) * ROW_ALIGN
    t = batch * lp
    tm_proj = _tile(t, (1280, 640))
    tm = _tile(t, (640,))

    h = jnp.concatenate([jnp.zeros((batch, N_PAD, d), x.dtype),
                         jnp.broadcast_to(meta.astype(x.dtype), (batch, N_META, d)),
                         x, jnp.zeros((batch, lp - l_real, d), x.dtype)], axis=1).reshape(t, d)

    v_first = None
    for l in range(DEPTH):
        vres_w = w_in_vres[l - 1] if l > 0 else jnp.zeros((d, R_MV_RANK), w_in.dtype)
        w_comb = _split_in_cols(w_in[l], vres_w).astype(BF16)
        p = _proj(h, ln1_w[l][None, :], w_comb, tm_proj, 640)

        o_h = _hgrn(p, hg_lb_logits, hg_norm_w[l].reshape(1, H_WIDTH), l, batch, lp)
        o_m = _mamba(p, m_conv_w[l], m_conv_b[l], m_dt_bias[l], m_a_log[l], m_d[l], m_norm_w[l], batch, lp)

        mu = r_mu[l][None, :]
        o = 3 * R_WIDTH
        prm = {
            "mu_rkv": mu[:, :o],
            "mu_wa": jnp.concatenate([_pad_cols(mu[:, o:o + R_DECAY_RANK], LANES),
                                      _pad_cols(mu[:, o + R_DECAY_RANK:o + R_DECAY_RANK + R_AAA_RANK], LANES)], axis=1),
            "mu_g1": mu[:, o + R_DECAY_RANK + R_AAA_RANK:],
            "w0": r_w0[l][None, :], "w2": _pad_rows(r_w2[l], LANES).astype(BF16),
            "a0": r_a0[l][None, :], "a2": _pad_rows(r_a2[l], LANES).astype(BF16),
            "g2": r_g2[l].astype(BF16),
            "k_k": r_k_k[l][None, :], "k_a": r_k_a[l][None, :], "r_k": r_r_k[l].reshape(1, R_WIDTH),
            "gn_w": r_gn_w[l][None, :], "gn_b": r_gn_b[l][None, :],
        }
        if l == 0:
            o_r, v_first = _rwkv(p, None, prm, True, batch, lp)
        else:
            prm["mu_vr"] = _pad_cols(r_mu_vres[l - 1][None, :], LANES)
            prm["v0"] = r_v0[l - 1][None, :]
            prm["v2"] = _pad_rows(r_v2[l - 1], LANES).astype(BF16)
            (o_r,) = _rwkv(p, v_first, prm, False, batch, lp)

        mixed = _merge(o_h, o_m, o_r, p, w_up_h[l].astype(BF16), w_up_m[l].astype(BF16),
                       w_up_r[l].astype(BF16), tm, 1024)
        h = _outproj(h, mixed, w_out[l].astype(BF16), tm, lp)
        h = _mlp(h, ln2_w[l][None, :], w_mlp_in[l].astype(BF16), w_mlp_out[l].astype(BF16),
                 lnf_w[None, :], tm, 512, lp, l == DEPTH - 1)

    return h.reshape(batch, lp, d)[:, CHUNK:l_real]
```

```python
import functools

import jax
import jax.numpy as jnp
from jax import lax
from jax.experimental import pallas as pl
from jax.experimental.pallas import tpu as pltpu

F32 = jnp.float32
BF16 = jnp.bfloat16

LANES = 128

D_MODEL = 2048
DEPTH = 2
N_META = 16
CHUNK = 64
N_PAD = CHUNK - N_META
MLP_HIDDEN = 4 * D_MODEL
NORM_EPS = 1e-6
N_BRANCH = 3
L2_EPS = 1e-24

H_WIDTH = D_MODEL // 2
H_HEAD_DIM = 128
H_HEADS = H_WIDTH // H_HEAD_DIM

M_WIDTH = D_MODEL // 2
M_HEAD_DIM = 64
M_HEADS = M_WIDTH // M_HEAD_DIM
M_GROUPS = 2
M_STATE = 128
M_CONV = 4
M_BC = M_GROUPS * M_STATE

R_WIDTH = D_MODEL // 2
R_HEAD_DIM = 64
R_HEADS = R_WIDTH // R_HEAD_DIM
R_DECAY_RANK = max(32, int(round(1.8 * D_MODEL ** 0.5 / 32)) * 32)
R_AAA_RANK = max(32, int(round(1.8 * D_MODEL ** 0.5 / 32)) * 32)
R_MV_RANK = max(32, int(round(1.3 * D_MODEL ** 0.5 / 32)) * 32)
R_GATE_RANK = max(32, int(round(0.6 * D_MODEL ** 0.8 / 32)) * 32)
R_GN_EPS = 64e-5
N_PAIRS = R_HEADS // 2

H_COLS = 4 * H_WIDTH
M_CONV_CH = M_WIDTH + 2 * M_BC
M_COLS = M_WIDTH + M_CONV_CH + M_HEADS
R_COLS = 3 * R_WIDTH + R_DECAY_RANK + R_AAA_RANK + R_GATE_RANK
GATE_COLS = N_BRANCH * D_MODEL
IN_COLS = H_COLS + M_COLS + R_COLS + GATE_COLS

C_H = 0
C_GATE = C_H + H_COLS
C_Z = C_GATE + GATE_COLS
C_X = C_Z + M_WIDTH
C_RKV = C_X + M_WIDTH
C_BC = C_RKV + 3 * R_WIDTH
C_G1 = C_BC + 2 * M_BC
C_WA = C_G1 + R_GATE_RANK
C_DT = C_WA + 2 * LANES
C_VRES = C_DT + LANES
P_COLS = C_VRES + LANES

ROW_ALIGN = 1280
SEQ_BLOCK = 256
VMEM_LIMIT = 56 * 1024 * 1024

NN = (((1,), (0,)), ((), ()))
NT = (((1,), (1,)), ((), ()))
TN = (((0,), (0,)), ((), ()))


def _dot(a, b, dims=NN):
    return lax.dot_general(a.astype(BF16), b.astype(BF16), dims, preferred_element_type=F32)


def _split3(x):
    hi = x.astype(BF16)
    r1 = x - hi.astype(F32)
    mid = r1.astype(BF16)
    lo = (r1 - mid.astype(F32)).astype(BF16)
    return hi, mid, lo


def _mask_dot_rhs(m, x):
    mb = m.astype(BF16)
    return jnp.dot(jnp.concatenate([mb, mb, mb], axis=1), jnp.concatenate(_split3(x), axis=0),
                   preferred_element_type=F32)


def _mask_dot_lhs(x, m):
    mb = m.astype(BF16)
    return jnp.dot(jnp.concatenate(_split3(x), axis=1), jnp.concatenate([mb, mb, mb], axis=0),
                   preferred_element_type=F32)


def _sigmoid(x):
    return 1.0 / (1.0 + jnp.exp(-x))


def _silu(x):
    return x * _sigmoid(x)


def _softplus(x):
    return jnp.maximum(x, 0.0) + jnp.log(1.0 + jnp.exp(-jnp.abs(x)))


def _iota(shape, dim):
    return lax.broadcasted_iota(jnp.int32, shape, dim)


def _tri(n):
    return (_iota((n, n), 0) >= _iota((n, n), 1)).astype(F32)


def _block_diag(x):
    lane = _iota(x.shape, 1)
    return jnp.concatenate([jnp.where(lane < 64, x, 0.0), jnp.where(lane >= 64, x, 0.0)], axis=0)


def _params(sem):
    return pltpu.CompilerParams(dimension_semantics=sem, vmem_limit_bytes=VMEM_LIMIT)


def _proj_kernel(h_ref, lnw_ref, w_ref, o_ref, u_ref):
    @pl.when(pl.program_id(1) == 0)
    def _():
        x = h_ref[...]
        ms = jnp.mean(x * x, axis=-1, keepdims=True)
        u_ref[...] = (x * lax.rsqrt(ms + NORM_EPS) * lnw_ref[...]).astype(BF16)

    o_ref[...] = jnp.dot(u_ref[...], w_ref[...], preferred_element_type=F32)


def _proj(h, lnw, w, tm, tn):
    t = h.shape[0]
    return pl.pallas_call(
        _proj_kernel,
        grid=(t // tm, P_COLS // tn),
        in_specs=[
            pl.BlockSpec((tm, D_MODEL), lambda i, j: (i, 0)),
            pl.BlockSpec((1, D_MODEL), lambda i, j: (0, 0)),
            pl.BlockSpec((D_MODEL, tn), lambda i, j: (0, j)),
        ],
        out_specs=pl.BlockSpec((tm, tn), lambda i, j: (i, j)),
        out_shape=jax.ShapeDtypeStruct((t, P_COLS), F32),
        scratch_shapes=[pltpu.VMEM((tm, D_MODEL), BF16)],
        compiler_params=_params(("parallel", "arbitrary")),
        name="in_proj",
    )(h, lnw, w)


HG_SUB = 16


def _hgrn_kernel(q_ref, f_ref, i_ref, g_ref, lbl_ref, nw_ref, o_ref,
                 st_ref, qs, ks, gs, vs, os_, *, layer, rb):
    jb = pl.program_id(1)

    @pl.when(jb == 0)
    def _():
        st_ref[...] = jnp.zeros(st_ref.shape, F32)

    lg = lbl_ref[...]
    mx = jnp.max(lg, axis=0, keepdims=True)
    ex = jnp.exp(lg - mx)
    sm = ex / jnp.sum(ex, axis=0, keepdims=True)
    lb = jnp.sum(sm[0:layer + 1], axis=0, keepdims=True) - sm[0:1]

    tri = _tri(CHUNK)
    row = _iota((CHUNK, 1), 0)
    row8 = _iota((8, 1), 0)
    heads = [slice(hd * H_HEAD_DIM, (hd + 1) * H_HEAD_DIM) for hd in range(H_HEADS)]
    arow = _iota((CHUNK, CHUNK), 0)
    acol = _iota((CHUNK, CHUNK), 1)
    off_mask = acol < (arow // HG_SUB) * HG_SUB
    n_sub = CHUNK // HG_SUB

    def chunk(c, carry):
        r0 = pl.multiple_of(c * CHUNK, CHUNK)
        rows = pl.ds(r0, CHUNK)
        valid = ((jb * rb + r0 + row) >= N_PAD).astype(F32)
        q = _silu(q_ref[rows, :])
        fr = f_ref[rows, :]
        logf = jnp.log(lb + (1.0 - lb) * _sigmoid(fr))
        k = (1.0 - lb) * _sigmoid(-fr) * valid
        v = i_ref[rows, :]
        gc = _mask_dot_rhs(tri, logf)
        qs[...] = q
        ks[...] = k
        gs[...] = gc
        vs[...] = v

        for sub in range(n_sub):
            base = sub * HG_SUB
            acc = [jnp.zeros((8, H_WIDTH), F32) for _ in range(HG_SUB // 8)]
            for j in range(HG_SUB):
                g_j = gs[base + j:base + j + 1, :]
                k_j = ks[base + j:base + j + 1, :]
                v_j = vs[base + j:base + j + 1, :]
                for half in range(j // 8, HG_SUB // 8):
                    h0 = base + 8 * half
                    e = jnp.exp(jnp.minimum(gs[h0:h0 + 8, :] - g_j, 0.0)) * qs[h0:h0 + 8, :] * k_j
                    parts = []
                    for hs in heads:
                        parts.append(jnp.sum(e[:, hs], axis=-1, keepdims=True) * v_j[:, hs])
                    contrib = jnp.concatenate(parts, axis=-1)
                    if half == j // 8:
                        contrib = jnp.where(row8 >= j - 8 * half, contrib, 0.0)
                    acc[half] = acc[half] + contrib
            for half in range(HG_SUB // 8):
                os_[base + 8 * half:base + 8 * half + 8, :] = acc[half]

        g_last = gc[CHUNK - 1:CHUNK, :]
        qg = q * jnp.exp(gc)
        kdec = k * jnp.exp(g_last - gc)
        e_last = jnp.exp(g_last)
        attn = []
        for hs in heads:
            g_h, q_h, k_h = gc[:, hs], q[:, hs], k[:, hs]
            qcat, kcat = [], []
            for sub in range(1, n_sub):
                base = sub * HG_SUB
                g_r = g_h[base:base + 1, :]
                q_sub = q_h[base:base + HG_SUB] * jnp.exp(jnp.minimum(g_h[base:base + HG_SUB] - g_r, 0.0))
                pieces = [jnp.zeros((base, H_HEAD_DIM), F32), q_sub]
                if base + HG_SUB < CHUNK:
                    pieces.append(jnp.zeros((CHUNK - base - HG_SUB, H_HEAD_DIM), F32))
                qcat.append(jnp.concatenate(pieces, axis=0))
                kcat.append(k_h * jnp.exp(jnp.minimum(g_r - g_h, 0.0)))
            a_h = _dot(jnp.concatenate(qcat, axis=1), jnp.concatenate(kcat, axis=1), NT)
            attn.append(jnp.where(off_mask, a_h, 0.0))
        sts = [st_ref[hd] for hd in range(H_HEADS)]
        inter = [_dot(qg[:, hs], sts[hd], NT) for hd, hs in enumerate(heads)]
        upd = [_dot(v[:, hs], kdec[:, hs], TN) for hs in heads]
        intra = [_dot(attn[hd], v[:, hs]) for hd, hs in enumerate(heads)]
        for hd, hs in enumerate(heads):
            os_[:, hs] = os_[:, hs] + intra[hd] + inter[hd]
            st_ref[hd] = sts[hd] * e_last[:, hs] + upd[hd]

        o = os_[...] * _sigmoid(g_ref[rows, :])
        outs = []
        for hd in range(H_HEADS):
            hs = slice(hd * H_HEAD_DIM, (hd + 1) * H_HEAD_DIM)
            o_h = o[:, hs]
            outs.append(o_h * lax.rsqrt(jnp.mean(o_h * o_h, axis=-1, keepdims=True) + NORM_EPS))
        o_ref[rows, :] = (jnp.concatenate(outs, axis=-1) * nw_ref[...]).astype(o_ref.dtype)
        return carry

    lax.fori_loop(0, rb // CHUNK, chunk, 0)


def _hgrn(p, lb_logits, norm_w, layer, batch, lp):
    rb = SEQ_BLOCK
    nb = lp // rb
    t = p.shape[0]
    cb = C_H // H_WIDTH

    def col(k):
        return pl.BlockSpec((rb, H_WIDTH), lambda b, j, k=k: (b * nb + j, cb + k))

    return pl.pallas_call(
        functools.partial(_hgrn_kernel, layer=layer, rb=rb),
        grid=(batch, nb),
        in_specs=[col(0), col(1), col(2), col(3),
                  pl.BlockSpec((DEPTH, H_WIDTH), lambda b, j: (0, 0)),
                  pl.BlockSpec((1, H_WIDTH), lambda b, j: (0, 0))],
        out_specs=pl.BlockSpec((rb, H_WIDTH), lambda b, j: (b * nb + j, 0)),
        out_shape=jax.ShapeDtypeStruct((t, H_WIDTH), BF16),
        scratch_shapes=[pltpu.VMEM((H_HEADS, H_HEAD_DIM, H_HEAD_DIM), F32)]
        + [pltpu.VMEM((CHUNK, H_WIDTH), F32) for _ in range(5)],
        compiler_params=_params(("parallel", "arbitrary")),
        name="hgrn2",
    )(p, p, p, p, lb_logits, norm_w)


HIST = 8


def _mamba_kernel(z_ref, x_ref, bc_ref, dt_ref, cwx_ref, cwb_ref, cbx_ref, cbb_ref,
                  dtb_ref, alog_ref, dsk_ref, nw_ref, o_ref,
                  st_ref, xe, be, xa, ba, *, rb):
    jb = pl.program_id(1)

    @pl.when(jb == 0)
    def _():
        st_ref[...] = jnp.zeros(st_ref.shape, F32)
        xe[0:HIST, :] = jnp.zeros((HIST, M_WIDTH), F32)
        be[0:HIST, :] = jnp.zeros((HIST, 2 * M_BC), F32)

    def conv(ext, src_ref, w_ref, b_ref, dst):
        ext[HIST:HIST + rb, :] = src_ref[...]
        acc = b_ref[...]
        for tap in range(M_CONV):
            o = HIST - (M_CONV - 1) + tap
            acc = acc + w_ref[tap:tap + 1, :] * ext[o:o + rb, :]
        dst[...] = _silu(acc)
        ext[0:HIST, :] = ext[rb:rb + HIST, :]

    conv(xe, x_ref, cwx_ref, cbx_ref, xa)
    conv(be, bc_ref, cwb_ref, cbb_ref, ba)

    tri = _tri(CHUNK)
    ones = jnp.ones((CHUNK, CHUNK), F32)
    row = _iota((CHUNK, 1), 0)
    expand = (_iota((LANES, M_WIDTH), 1) // M_HEAD_DIM == _iota((LANES, M_WIDTH), 0)).astype(F32)
    pos_j = _iota((CHUNK, M_WIDTH), 1) % CHUNK
    row_i = _iota((CHUNK, M_WIDTH), 0)
    a_neg = -jnp.exp(alog_ref[...])
    hpg = M_HEADS // M_GROUPS

    def chunk(c, carry):
        r0 = pl.multiple_of(c * CHUNK, CHUNK)
        rows = pl.ds(r0, CHUNK)
        valid = ((jb * rb + r0 + row) >= N_PAD).astype(F32)
        xs = xa[rows, :]
        bcm = ba[rows, :]
        dt = _softplus(dt_ref[rows, :] + dtb_ref[...]) * valid
        dt_e = _mask_dot_lhs(dt, expand)
        da_e = _mask_dot_lhs(dt * a_neg, expand)
        acum = _mask_dot_rhs(tri, da_e)
        acum_j = _mask_dot_rhs(ones, jnp.where(row_i <= pos_j, da_e, 0.0))
        lmat = jnp.where(row_i >= pos_j, jnp.exp(jnp.minimum(acum - acum_j, 0.0)), 0.0)
        a_last = acum[CHUNK - 1:CHUNK, :]
        e_cum = jnp.exp(acum)
        e_end = jnp.exp(a_last - acum)
        e_last = jnp.exp(a_last)
        xc = xs * dt_e
        scores = []
        for g in range(M_GROUPS):
            b_g = bcm[:, g * M_STATE:(g + 1) * M_STATE]
            c_g = bcm[:, M_BC + g * M_STATE:M_BC + (g + 1) * M_STATE]
            scores.append(_dot(c_g, jnp.concatenate([b_g] * hpg, axis=0), NT))
        attn = jnp.concatenate(scores, axis=1) * lmat
        ys = []
        for pr in range(M_HEADS // 2):
            ps = slice(pr * LANES, (pr + 1) * LANES)
            g = (2 * pr) // hpg
            b_g = bcm[:, g * M_STATE:(g + 1) * M_STATE]
            c_g = bcm[:, M_BC + g * M_STATE:M_BC + (g + 1) * M_STATE]
            xc_p = xc[:, ps]
            st = st_ref[pr]
            y = _dot(attn[:, ps], _block_diag(xc_p)) + _dot(c_g, st) * e_cum[:, ps]
            st_ref[pr] = st * e_last[:, ps] + _dot(b_g, xc_p * e_end[:, ps], TN)
            ys.append(y)
        y = (jnp.concatenate(ys, axis=1) + dsk_ref[...] * xs) * _silu(z_ref[rows, :])
        gw = M_WIDTH // M_GROUPS
        outs = []
        for g in range(M_GROUPS):
            yg = y[:, g * gw:(g + 1) * gw]
            outs.append(yg * lax.rsqrt(jnp.mean(yg * yg, axis=-1, keepdims=True) + NORM_EPS))
        o_ref[rows, :] = (jnp.concatenate(outs, axis=1) * nw_ref[...]).astype(o_ref.dtype)
        return carry

    lax.fori_loop(0, rb // CHUNK, chunk, 0)


def _mamba(p, conv_w, conv_b, dt_bias, a_log, d_skip, norm_w, batch, lp):
    rb = SEQ_BLOCK
    nb = lp // rb
    t = p.shape[0]

    def col(off, width):
        return pl.BlockSpec((rb, width), lambda b, j: (b * nb + j, off // width))

    def full(shape):
        return pl.BlockSpec(shape, lambda b, j: (0,) * len(shape))

    pad = LANES - M_HEADS
    dtb = jnp.pad(dt_bias.reshape(1, M_HEADS), ((0, 0), (0, pad)))
    alog = jnp.pad(a_log.reshape(1, M_HEADS), ((0, 0), (0, pad)))
    dsk = jnp.repeat(d_skip, M_HEAD_DIM).reshape(1, M_WIDTH)
    return pl.pallas_call(
        functools.partial(_mamba_kernel, rb=rb),
        grid=(batch, nb),
        in_specs=[col(C_Z, M_WIDTH), col(C_X, M_WIDTH), col(C_BC, 2 * M_BC), col(C_DT, LANES),
                  full((M_CONV, M_WIDTH)), full((M_CONV, 2 * M_BC)),
                  full((1, M_WIDTH)), full((1, 2 * M_BC)),
                  full((1, LANES)), full((1, LANES)), full((1, M_WIDTH)), full((1, M_WIDTH))],
        out_specs=pl.BlockSpec((rb, M_WIDTH), lambda b, j: (b * nb + j, 0)),
        out_shape=jax.ShapeDtypeStruct((t, M_WIDTH), BF16),
        scratch_shapes=[pltpu.VMEM((M_HEADS // 2, M_STATE, LANES), F32),
                        pltpu.VMEM((rb + HIST, M_WIDTH), F32),
                        pltpu.VMEM((rb + HIST, 2 * M_BC), F32),
                        pltpu.VMEM((rb, M_WIDTH), F32),
                        pltpu.VMEM((rb, 2 * M_BC), F32)],
        compiler_params=_params(("parallel", "arbitrary")),
        name="mamba2",
    )(p, p, p, p, conv_w[:, :M_WIDTH], conv_w[:, M_WIDTH:], conv_b[None, :M_WIDTH],
      conv_b[None, M_WIDTH:], dtb, alog, dsk, norm_w[None, :])


R_SHIFT_W = 3 * R_WIDTH


def _rwkv_kernel(*refs, first, rb):
    if first:
        (rkv_ref, g1_ref, wa_ref, mu_rkv, mu_g1, mu_wa,
         w0_ref, w2_ref, a0_ref, a2_ref, g2_ref, kk_ref, ka_ref, rk_ref, gnw_ref, gnb_ref,
         o_ref, vf_out,
         st_ref, e_rkv, e_g1, e_wa, dec_s,
         lw_s, r_s, k_s, v_s, a_s, b_s, o_s, g_s, rt_s, wt_s, uv_s, arb_s, ov_s, bh_s, kh_s) = refs
    else:
        (rkv_ref, g1_ref, wa_ref, vr_ref, vf_ref, mu_rkv, mu_g1, mu_wa, mu_vr,
         w0_ref, w2_ref, a0_ref, a2_ref, g2_ref, kk_ref, ka_ref, rk_ref, gnw_ref, gnb_ref,
         v0_ref, v2_ref,
         o_ref,
         st_ref, e_rkv, e_g1, e_wa, e_vr, dec_s,
         lw_s, r_s, k_s, v_s, a_s, b_s, o_s, g_s, rt_s, wt_s, uv_s, arb_s, ov_s, bh_s, kh_s) = refs
    jb = pl.program_id(1)
    exts = [e_rkv, e_g1, e_wa] + ([] if first else [e_vr])

    @pl.when(jb == 0)
    def _():
        st_ref[...] = jnp.zeros(st_ref.shape, F32)
        for ext in exts:
            ext[0:HIST, :] = jnp.zeros((HIST, ext.shape[1]), F32)

    def shift(ext, src_ref, mu_ref):
        ext[HIST:HIST + rb, :] = src_ref[...]
        cur = ext[HIST:HIST + rb, :]
        prev = ext[HIST - 1:HIST - 1 + rb, :]
        out = cur + (prev - cur) * mu_ref[...]
        ext[0:HIST, :] = ext[rb:rb + HIST, :]
        return out

    rkv = shift(e_rkv, rkv_ref, mu_rkv)
    gl = shift(e_g1, g1_ref, mu_g1)
    wa = shift(e_wa, wa_ref, mu_wa)
    r = rkv[:, 0:R_WIDTH]
    k = rkv[:, R_WIDTH:2 * R_WIDTH]
    v = rkv[:, 2 * R_WIDTH:3 * R_WIDTH]
    wl = wa[:, 0:LANES]
    al = wa[:, LANES:2 * LANES]

    valid = ((jb * rb + _iota((rb, 1), 0)) >= N_PAD).astype(F32)
    w_log = -_softplus(-(w0_ref[...] + _dot(jnp.tanh(wl), w2_ref[...]))) - 0.5
    lw_s[...] = -jnp.exp(w_log)
    a = _sigmoid(a0_ref[...] + _dot(al, a2_ref[...]))
    if first:
        vf_out[...] = v
    else:
        vl = shift(e_vr, vr_ref, mu_vr)
        v = v + (vf_ref[...] - v) * _sigmoid(v0_ref[...] + _dot(vl, v2_ref[...]))
    g_s[...] = _dot(_sigmoid(gl), g2_ref[...])

    seg_ones = (_iota((LANES, LANES), 0) // R_HEAD_DIM == _iota((LANES, LANES), 1) // R_HEAD_DIM).astype(F32)

    def head_sum(x):
        return jnp.concatenate(
            [_mask_dot_lhs(x[:, s * LANES:(s + 1) * LANES], seg_ones) for s in range(N_PAIRS)], axis=1)

    kk = k * kk_ref[...]
    kk = kk * lax.rsqrt(jnp.maximum(head_sum(kk * kk), L2_EPS))
    kh = k * (1.0 + (a - 1.0) * ka_ref[...]) * valid
    r_s[...] = r
    k_s[...] = kh
    v_s[...] = v
    a_s[...] = -kk
    b_s[...] = kk * a

    tri = _tri(CHUNK)
    t_i = _iota((CHUNK, LANES), 0)
    s_j = _iota((CHUNK, LANES), 1) % CHUNK
    strict = t_i > s_j
    incl = t_i >= s_j
    eye = (t_i == s_j).astype(F32)
    same_head = _iota((LANES, LANES), 0) // R_HEAD_DIM == _iota((LANES, LANES), 1) // R_HEAD_DIM
    n_double = 5
    assert 2 ** (n_double + 1) == CHUNK

    pairs = [slice(pr * LANES, (pr + 1) * LANES) for pr in range(N_PAIRS)]

    def precompute(c, carry):
        rows = pl.ds(pl.multiple_of(c * CHUNK, CHUNK), CHUNK)
        lw = lw_s[rows, :]
        cum = _mask_dot_rhs(tri, lw)
        c_last = cum[CHUNK - 1:CHUNK, :]
        e_inv = jnp.exp(-cum)
        e_end = jnp.exp(c_last - cum)
        b_c, k_c, v_c = b_s[rows, :], k_s[rows, :], v_s[rows, :]
        a_t = a_s[rows, :] * jnp.exp(cum - lw)
        r_t = r_s[rows, :] * jnp.exp(cum)
        b_t = b_c * e_inv
        k_t = k_c * e_inv
        bh_s[rows, :] = b_c * e_end
        kh_s[rows, :] = k_c * e_end
        rt_s[rows, :] = r_t
        dec_s[pl.ds(c, 1), :] = jnp.exp(c_last)
        m = [_dot(jnp.concatenate([a_t[:, ps], r_t[:, ps]], axis=0),
                  jnp.concatenate([_block_diag(b_t[:, ps]), _block_diag(k_t[:, ps])], axis=0), NT)
             for ps in pairs]
        a_ab = [jnp.where(strict, x[0:CHUNK, 0:LANES], 0.0) for x in m]
        a_ak = [jnp.where(strict, x[0:CHUNK, LANES:2 * LANES], 0.0) for x in m]
        a_rk = [jnp.where(incl, x[CHUNK:2 * CHUNK, LANES:2 * LANES], 0.0) for x in m]
        for x, ps in zip(m, pairs):
            arb_s[rows, ps] = jnp.where(incl, x[CHUNK:2 * CHUNK, 0:LANES], 0.0)
        inv = [eye + x for x in a_ab]
        pw = [_dot(x, _block_diag(x)) for x in a_ab]
        for it in range(n_double - 1):
            both = [_dot(jnp.concatenate([p_, i_], axis=0), _block_diag(p_)) for p_, i_ in zip(pw, inv)]
            pw = [x[0:CHUNK] for x in both]
            inv = [i_ + x[CHUNK:2 * CHUNK] for i_, x in zip(inv, both)]
        inv = [i_ + _dot(i_, _block_diag(p_)) for i_, p_ in zip(inv, pw)]
        vb = [_block_diag(v_c[:, ps]) for ps in pairs]
        wt = [_dot(i_, _block_diag(a_t[:, ps])) for i_, ps in zip(inv, pairs)]
        t1 = [_dot(x, b_) for x, b_ in zip(a_ak, vb)]
        ov = [_dot(x, b_) for x, b_ in zip(a_rk, vb)]
        uv = [_dot(i_, _block_diag(x)) for i_, x in zip(inv, t1)]
        for pr, ps in enumerate(pairs):
            wt_s[rows, ps] = wt[pr]
            uv_s[rows, ps] = uv[pr]
            ov_s[rows, ps] = ov[pr]
        return carry

    lax.fori_loop(0, rb // CHUNK, precompute, 0)

    def recur(c, carry):
        rows = pl.ds(pl.multiple_of(c * CHUNK, CHUNK), CHUNK)
        dec = dec_s[pl.ds(c, 1), :]
        sts = [st_ref[pr] for pr in range(N_PAIRS)]
        u = [_dot(wt_s[rows, ps], st, NT) + uv_s[rows, ps] for ps, st in zip(pairs, sts)]
        o1 = [_dot(rt_s[rows, ps], st, NT) for ps, st in zip(pairs, sts)]
        upd = [_dot(jnp.concatenate([u_, v_s[rows, ps]], axis=0),
                    jnp.concatenate([bh_s[rows, ps], kh_s[rows, ps]], axis=0), TN)
               for ps, u_ in zip(pairs, u)]
        o2 = [_dot(arb_s[rows, ps], _block_diag(u_)) for ps, u_ in zip(pairs, u)]
        for pr, ps in enumerate(pairs):
            st_ref[pr] = sts[pr] * dec[:, ps] + jnp.where(same_head, upd[pr], 0.0)
            o_s[rows, ps] = o1[pr] + o2[pr] + ov_s[rows, ps]
        return carry

    lax.fori_loop(0, rb // CHUNK, recur, 0)

    o = o_s[...]
    inv_n = 1.0 / R_HEAD_DIM
    mu = head_sum(o) * inv_n
    d = o - mu
    var = head_sum(d * d) * inv_n
    o = d * lax.rsqrt(var + R_GN_EPS) * gnw_ref[...] + gnb_ref[...]
    o = o + head_sum(r_s[...] * k_s[...] * rk_ref[...]) * v_s[...]
    o_ref[...] = (o * g_s[...]).astype(o_ref.dtype)


def _rwkv(p, v_first, prm, first, batch, lp):
    rb = SEQ_BLOCK
    nb = lp // rb
    t = p.shape[0]

    def col(off, width):
        return pl.BlockSpec((rb, width), lambda b, j: (b * nb + j, off // width))

    def full(a):
        return pl.BlockSpec(a.shape, lambda b, j: (0,) * a.ndim)

    row_spec = pl.BlockSpec((rb, R_WIDTH), lambda b, j: (b * nb + j, 0))
    acts = [p, p, p]
    act_specs = [col(C_RKV, R_SHIFT_W), col(C_G1, R_GATE_RANK), col(C_WA, 2 * LANES)]
    mus = [prm["mu_rkv"], prm["mu_g1"], prm["mu_wa"]]
    tail = []
    if not first:
        acts += [p, v_first]
        act_specs += [col(C_VRES, LANES), row_spec]
        mus.append(prm["mu_vr"])
        tail = [prm["v0"], prm["v2"]]
    consts = mus + [prm[n] for n in ("w0", "w2", "a0", "a2", "g2", "k_k", "k_a", "r_k", "gn_w", "gn_b")] + tail
    out_shape = [jax.ShapeDtypeStruct((t, R_WIDTH), BF16)]
    out_specs = [row_spec]
    if first:
        out_shape.append(jax.ShapeDtypeStruct((t, R_WIDTH), F32))
        out_specs.append(row_spec)
    ext_w = [R_SHIFT_W, R_GATE_RANK, 2 * LANES] + ([] if first else [LANES])
    scratch = ([pltpu.VMEM((N_PAIRS, LANES, LANES), F32)]
               + [pltpu.VMEM((rb + HIST, w), F32) for w in ext_w]
               + [pltpu.VMEM((max(rb // CHUNK, 8), R_WIDTH), F32)]
               + [pltpu.VMEM((rb, R_WIDTH), F32) for _ in range(15)])
    return pl.pallas_call(
        functools.partial(_rwkv_kernel, first=first, rb=rb),
        grid=(batch, nb),
        in_specs=act_specs + [full(a) for a in consts],
        out_specs=out_specs,
        out_shape=out_shape,
        scratch_shapes=scratch,
        compiler_params=_params(("parallel", "arbitrary")),
        name="rwkv7",
    )(*acts, *consts)


def _merge_kernel(oh_ref, om_ref, or_ref, g0_ref, g1_ref, g2_ref, wh_ref, wm_ref, wr_ref, o_ref):
    acc = _sigmoid(g0_ref[...]) * jnp.dot(oh_ref[...], wh_ref[...], preferred_element_type=F32)
    acc = acc + _sigmoid(g1_ref[...]) * jnp.dot(om_ref[...], wm_ref[...], preferred_element_type=F32)
    acc = acc + _sigmoid(g2_ref[...]) * jnp.dot(or_ref[...], wr_ref[...], preferred_element_type=F32)
    o_ref[...] = acc.astype(o_ref.dtype)


def _merge(o_h, o_m, o_r, p, w_h, w_m, w_r, tm, tn):
    t = p.shape[0]
    gb = C_GATE // tn
    nt = D_MODEL // tn

    def act():
        return pl.BlockSpec((tm, H_WIDTH), lambda i, j: (i, 0))

    def gate(k):
        return pl.BlockSpec((tm, tn), lambda i, j, k=k: (i, gb + k * nt + j))

    def wt():
        return pl.BlockSpec((H_WIDTH, tn), lambda i, j: (0, j))

    return pl.pallas_call(
        _merge_kernel,
        grid=(t // tm, nt),
        in_specs=[act(), act(), act(), gate(0), gate(1), gate(2), wt(), wt(), wt()],
        out_specs=pl.BlockSpec((tm, tn), lambda i, j: (i, j)),
        out_shape=jax.ShapeDtypeStruct((t, D_MODEL), BF16),
        compiler_params=_params(("parallel", "arbitrary")),
        name="merge",
    )(o_h, o_m, o_r, p, p, p, w_h, w_m, w_r)


def _row_valid(tm, lp):
    pos = (pl.program_id(0) * tm + _iota((tm, 1), 0)) % lp
    return (pos >= N_PAD).astype(F32)


def _outproj_kernel(h_ref, m_ref, w_ref, o_ref, *, tm, lp):
    upd = jnp.dot(m_ref[...], w_ref[...], preferred_element_type=F32)
    o_ref[...] = h_ref[...] + _row_valid(tm, lp) * upd


def _outproj(h, mixed, w_out, tm, lp):
    t = h.shape[0]
    return pl.pallas_call(
        functools.partial(_outproj_kernel, tm=tm, lp=lp),
        grid=(t // tm,),
        in_specs=[pl.BlockSpec((tm, D_MODEL), lambda i: (i, 0)),
                  pl.BlockSpec((tm, D_MODEL), lambda i: (i, 0)),
                  pl.BlockSpec((D_MODEL, D_MODEL), lambda i: (0, 0))],
        out_specs=pl.BlockSpec((tm, D_MODEL), lambda i: (i, 0)),
        out_shape=jax.ShapeDtypeStruct((t, D_MODEL), F32),
        compiler_params=_params(("parallel",)),
        name="out_proj",
    )(h, mixed, w_out)


def _mlp_kernel(h_ref, lnw_ref, w1_ref, w2_ref, lnf_ref, o_ref, u_ref, *, tm, lp, final):
    kk = pl.program_id(1)

    @pl.when(kk == 0)
    def _():
        x = h_ref[...]
        ms = jnp.mean(x * x, axis=-1, keepdims=True)
        u_ref[...] = (x * lax.rsqrt(ms + NORM_EPS) * lnw_ref[...]).astype(BF16)
        o_ref[...] = jnp.zeros(o_ref.shape, F32)

    a = jnp.maximum(jnp.dot(u_ref[...], w1_ref[...], preferred_element_type=F32), 0.0)
    o_ref[...] += jnp.dot((a * a).astype(BF16), w2_ref[...], preferred_element_type=F32)

    @pl.when(kk == pl.num_programs(1) - 1)
    def _():
        y = h_ref[...] + _row_valid(tm, lp) * o_ref[...]
        if final:
            ms = jnp.mean(y * y, axis=-1, keepdims=True)
            y = y * lax.rsqrt(ms + NORM_EPS) * lnf_ref[...]
        o_ref[...] = y


def _mlp(h, lnw, w1, w2, lnf, tm, th, lp, final):
    t = h.shape[0]
    return pl.pallas_call(
        functools.partial(_mlp_kernel, tm=tm, lp=lp, final=final),
        grid=(t // tm, MLP_HIDDEN // th),
        in_specs=[pl.BlockSpec((tm, D_MODEL), lambda i, k: (i, 0)),
                  pl.BlockSpec((1, D_MODEL), lambda i, k: (0, 0)),
                  pl.BlockSpec((D_MODEL, th), lambda i, k: (0, k)),
                  pl.BlockSpec((th, D_MODEL), lambda i, k: (k, 0)),
                  pl.BlockSpec((1, D_MODEL), lambda i, k: (0, 0))],
        out_specs=pl.BlockSpec((tm, D_MODEL), lambda i, k: (i, 0)),
        out_shape=jax.ShapeDtypeStruct((t, D_MODEL), F32),
        scratch_shapes=[pltpu.VMEM((tm, D_MODEL), BF16)],
        compiler_params=_params(("parallel", "arbitrary")),
        name="mlp",
    )(h, lnw, w1, w2, lnf)


def _pad_cols(a, width):
    return jnp.pad(a, ((0, 0), (0, width - a.shape[1])))


def _pad_rows(a, height):
    return jnp.pad(a, ((0, height - a.shape[0]), (0, 0)))


def _split_in_cols(a, a_vres):
    o = 0
    hcols = a[:, o:o + H_COLS]; o += H_COLS
    z = a[:, o:o + M_WIDTH]; o += M_WIDTH
    xm = a[:, o:o + M_WIDTH]; o += M_WIDTH
    bc = a[:, o:o + 2 * M_BC]; o += 2 * M_BC
    dt = a[:, o:o + M_HEADS]; o += M_HEADS
    rkv = a[:, o:o + 3 * R_WIDTH]; o += 3 * R_WIDTH
    w1 = a[:, o:o + R_DECAY_RANK]; o += R_DECAY_RANK
    a1 = a[:, o:o + R_AAA_RANK]; o += R_AAA_RANK
    g1 = a[:, o:o + R_GATE_RANK]; o += R_GATE_RANK
    gates = a[:, o:o + GATE_COLS]; o += GATE_COLS
    assert o == IN_COLS
    out = jnp.concatenate([hcols, gates, z, xm, rkv, bc, g1, _pad_cols(w1, LANES), _pad_cols(a1, LANES),
                           _pad_cols(dt, LANES), _pad_cols(a_vres, LANES)], axis=1)
    assert out.shape[1] == P_COLS
    return out


def _tile(t, candidates):
    for c in candidates:
        if t % c == 0:
            return c
    raise ValueError(f"no tile for {t}")


def kernel(x, meta, ln1_w, ln2_w, lnf_w, w_in, w_in_vres, hg_lb_logits, hg_norm_w, m_conv_w, m_conv_b, m_dt_bias, m_a_log, m_d, m_norm_w, r_mu, r_mu_vres, r_w0, r_w2, r_a0, r_a2, r_v0, r_v2, r_g2, r_k_k, r_k_a, r_r_k, r_gn_w, r_gn_b, w_up_h, w_up_m, w_up_r, w_out, w_mlp_in, w_mlp_out):
    batch, seq, d = x.shape
    assert d == D_MODEL and M_HEAD_DIM == CHUNK and R_HEAD_DIM == CHUNK
    l_real = seq + CHUNK
    lp = -(-l_real // ROW_ALIGN) * ROW_ALIGN
    t = batch * lp
    tm_proj = _tile(t, (1280, 640))
    tm = _tile(t, (640,))

    h = jnp.concatenate([jnp.zeros((batch, N_PAD, d), x.dtype),
                         jnp.broadcast_to(meta.astype(x.dtype), (batch, N_META, d)),
                         x, jnp.zeros((batch, lp - l_real, d), x.dtype)], axis=1).reshape(t, d)

    v_first = None
    for l in range(DEPTH):
        vres_w = w_in_vres[l - 1] if l > 0 else jnp.zeros((d, R_MV_RANK), w_in.dtype)
        w_comb = _split_in_cols(w_in[l], vres_w).astype(BF16)
        p = _proj(h, ln1_w[l][None, :], w_comb, tm_proj, 1280)

        o_h = _hgrn(p, hg_lb_logits, hg_norm_w[l].reshape(1, H_WIDTH), l, batch, lp)
        o_m = _mamba(p, m_conv_w[l], m_conv_b[l], m_dt_bias[l], m_a_log[l], m_d[l], m_norm_w[l], batch, lp)

        mu = r_mu[l][None, :]
        o = 3 * R_WIDTH
        prm = {
            "mu_rkv": mu[:, :o],
            "mu_wa": jnp.concatenate([_pad_cols(mu[:, o:o + R_DECAY_RANK], LANES),
                                      _pad_cols(mu[:, o + R_DECAY_RANK:o + R_DECAY_RANK + R_AAA_RANK], LANES)], axis=1),
            "mu_g1": mu[:, o + R_DECAY_RANK + R_AAA_RANK:],
            "w0": r_w0[l][None, :], "w2": _pad_rows(r_w2[l], LANES).astype(BF16),
            "a0": r_a0[l][None, :], "a2": _pad_rows(r_a2[l], LANES).astype(BF16),
            "g2": r_g2[l].astype(BF16),
            "k_k": r_k_k[l][None, :], "k_a": r_k_a[l][None, :], "r_k": r_r_k[l].reshape(1, R_WIDTH),
            "gn_w": r_gn_w[l][None, :], "gn_b": r_gn_b[l][None, :],
        }
        if l == 0:
            o_r, v_first = _rwkv(p, None, prm, True, batch, lp)
        else:
            prm["mu_vr"] = _pad_cols(r_mu_vres[l - 1][None, :], LANES)
            prm["v0"] = r_v0[l - 1][None, :]
            prm["v2"] = _pad_rows(r_v2[l - 1], LANES).astype(BF16)
            (o_r,) = _rwkv(p, v_first, prm, False, batch, lp)

        mixed = _merge(o_h, o_m, o_r, p, w_up_h[l].astype(BF16), w_up_m[l].astype(BF16),
                       w_up_r[l].astype(BF16), tm, 1024)
        h = _outproj(h, mixed, w_out[l].astype(BF16), tm, lp)
        h = _mlp(h, ln2_w[l][None, :], w_mlp_in[l].astype(BF16), w_mlp_out[l].astype(BF16),
                 lnf_w[None, :], tm, 1024, lp, l == DEPTH - 1)

    return h.reshape(batch, lp, d)[:, CHUNK:l_real]
```

```python
import functools

import jax
import jax.numpy as jnp
from jax import lax
from jax.experimental import pallas as pl
from jax.experimental.pallas import tpu as pltpu

F32 = jnp.float32
BF16 = jnp.bfloat16

LANES = 128

D_MODEL = 2048
DEPTH = 2
N_META = 16
CHUNK = 64
N_PAD = CHUNK - N_META
MLP_HIDDEN = 4 * D_MODEL
NORM_EPS = 1e-6
N_BRANCH = 3
L2_EPS = 1e-24

H_WIDTH = D_MODEL // 2
H_HEAD_DIM = 128
H_HEADS = H_WIDTH // H_HEAD_DIM

M_WIDTH = D_MODEL // 2
M_HEAD_DIM = 64
M_HEADS = M_WIDTH // M_HEAD_DIM
M_GROUPS = 2
M_STATE = 128
M_CONV = 4
M_BC = M_GROUPS * M_STATE

R_WIDTH = D_MODEL // 2
R_HEAD_DIM = 64
R_HEADS = R_WIDTH // R_HEAD_DIM
R_DECAY_RANK = max(32, int(round(1.8 * D_MODEL ** 0.5 / 32)) * 32)
R_AAA_RANK = max(32, int(round(1.8 * D_MODEL ** 0.5 / 32)) * 32)
R_MV_RANK = max(32, int(round(1.3 * D_MODEL ** 0.5 / 32)) * 32)
R_GATE_RANK = max(32, int(round(0.6 * D_MODEL ** 0.8 / 32)) * 32)
R_GN_EPS = 64e-5
N_PAIRS = R_HEADS // 2

H_COLS = 4 * H_WIDTH
M_CONV_CH = M_WIDTH + 2 * M_BC
M_COLS = M_WIDTH + M_CONV_CH + M_HEADS
R_COLS = 3 * R_WIDTH + R_DECAY_RANK + R_AAA_RANK + R_GATE_RANK
GATE_COLS = N_BRANCH * D_MODEL
IN_COLS = H_COLS + M_COLS + R_COLS + GATE_COLS

C_H = 0
C_GATE = C_H + H_COLS
C_Z = C_GATE + GATE_COLS
C_X = C_Z + M_WIDTH
C_RKV = C_X + M_WIDTH
C_BC = C_RKV + 3 * R_WIDTH
C_G1 = C_BC + 2 * M_BC
C_WA = C_G1 + R_GATE_RANK
C_DT = C_WA + 2 * LANES
C_VRES = C_DT + LANES
P_COLS = C_VRES + LANES

ROW_ALIGN = 1280
SEQ_BLOCK = 256
VMEM_LIMIT = 56 * 1024 * 1024

NN = (((1,), (0,)), ((), ()))
NT = (((1,), (1,)), ((), ()))
TN = (((0,), (0,)), ((), ()))


def _dot(a, b, dims=NN):
    return lax.dot_general(a.astype(BF16), b.astype(BF16), dims, preferred_element_type=F32)


def _split(x, terms):
    out = []
    for _ in range(terms - 1):
        hi = x.astype(BF16)
        out.append(hi)
        x = x - hi.astype(F32)
    out.append(x.astype(BF16))
    return out


def _mask_dot_rhs(m, x, terms=3):
    mb = m.astype(BF16)
    return jnp.dot(jnp.concatenate([mb] * terms, axis=1), jnp.concatenate(_split(x, terms), axis=0),
                   preferred_element_type=F32)


def _mask_dot_lhs(x, m, terms=3):
    mb = m.astype(BF16)
    return jnp.dot(jnp.concatenate(_split(x, terms), axis=1), jnp.concatenate([mb] * terms, axis=0),
                   preferred_element_type=F32)


def _sigmoid(x):
    return 1.0 / (1.0 + jnp.exp(-x))


def _silu(x):
    return x * _sigmoid(x)


def _softplus(x):
    return jnp.maximum(x, 0.0) + jnp.log(1.0 + jnp.exp(-jnp.abs(x)))


def _iota(shape, dim):
    return lax.broadcasted_iota(jnp.int32, shape, dim)


def _tri(n):
    return (_iota((n, n), 0) >= _iota((n, n), 1)).astype(F32)


def _block_diag(x):
    lane = _iota(x.shape, 1)
    return jnp.concatenate([jnp.where(lane < 64, x, 0.0), jnp.where(lane >= 64, x, 0.0)], axis=0)


def _params(sem):
    return pltpu.CompilerParams(dimension_semantics=sem, vmem_limit_bytes=VMEM_LIMIT)


def _proj_kernel(h_ref, lnw_ref, w_ref, o_ref, u_ref):
    @pl.when(pl.program_id(1) == 0)
    def _():
        x = h_ref[...]
        ms = jnp.mean(x * x, axis=-1, keepdims=True)
        u_ref[...] = (x * lax.rsqrt(ms + NORM_EPS) * lnw_ref[...]).astype(BF16)

    o_ref[...] = jnp.dot(u_ref[...], w_ref[...], preferred_element_type=F32).astype(o_ref.dtype)


def _proj(h, lnw, w, tm, tn):
    t = h.shape[0]
    return pl.pallas_call(
        _proj_kernel,
        grid=(t // tm, P_COLS // tn),
        in_specs=[
            pl.BlockSpec((tm, D_MODEL), lambda i, j: (i, 0)),
            pl.BlockSpec((1, D_MODEL), lambda i, j: (0, 0)),
            pl.BlockSpec((D_MODEL, tn), lambda i, j: (0, j)),
        ],
        out_specs=pl.BlockSpec((tm, tn), lambda i, j: (i, j)),
        out_shape=jax.ShapeDtypeStruct((t, P_COLS), BF16),
        scratch_shapes=[pltpu.VMEM((tm, D_MODEL), BF16)],
        compiler_params=_params(("parallel", "arbitrary")),
        name="in_proj",
    )(h, lnw, w)


HG_SUB = 16
LOG2E = 1.4426950408889634


def _hgrn_kernel(q_ref, f_ref, i_ref, g_ref, lbl_ref, nw_ref, o_ref,
                 st_ref, qs, ks, gs, *, layer, rb):
    jb = pl.program_id(1)

    @pl.when(jb == 0)
    def _():
        st_ref[...] = jnp.zeros(st_ref.shape, F32)

    lg = lbl_ref[...]
    mx = jnp.max(lg, axis=0, keepdims=True)
    ex = jnp.exp(lg - mx)
    sm = ex / jnp.sum(ex, axis=0, keepdims=True)
    lb = jnp.sum(sm[0:layer + 1], axis=0, keepdims=True) - sm[0:1]

    tri = _tri(CHUNK)
    row = _iota((CHUNK, 1), 0)
    col8 = _iota((8, H_HEAD_DIM), 1)
    heads = [slice(hd * H_HEAD_DIM, (hd + 1) * H_HEAD_DIM) for hd in range(H_HEADS)]
    arow = _iota((CHUNK, CHUNK), 0)
    acol = _iota((CHUNK, CHUNK), 1)
    off_mask = acol < (arow // HG_SUB) * HG_SUB
    causal = arow >= acol
    n_sub = CHUNK // HG_SUB

    def chunk(c, carry):
        r0 = pl.multiple_of(c * CHUNK, CHUNK)
        rows = pl.ds(r0, CHUNK)
        valid = ((jb * rb + r0 + row) >= N_PAD).astype(F32)
        q = _silu(q_ref[rows, :].astype(F32))
        fr = f_ref[rows, :].astype(F32)
        logf = jnp.log(lb + (1.0 - lb) * _sigmoid(fr))
        k = (1.0 - lb) * _sigmoid(-fr) * valid
        v = i_ref[rows, :].astype(F32)
        g2 = _mask_dot_rhs(tri, logf) * LOG2E
        qs[...] = q
        ks[...] = k
        gs[...] = g2

        diag = [[None] * (CHUNK // 8) for _ in heads]
        for sub in range(n_sub):
            base = sub * HG_SUB
            acc = [[jnp.zeros((8, H_HEAD_DIM), F32) for _ in heads] for _ in range(HG_SUB // 8)]
            for j in range(HG_SUB):
                g_j = jnp.broadcast_to(gs[base + j:base + j + 1, :], (8, H_WIDTH))
                k_j = jnp.broadcast_to(ks[base + j:base + j + 1, :], (8, H_WIDTH))
                for half in range(j // 8, HG_SUB // 8):
                    h0 = base + 8 * half
                    d = gs[h0:h0 + 8, :] - g_j
                    if half == j // 8:
                        d = jnp.minimum(d, 0.0)
                    e = jnp.exp2(d) * (qs[h0:h0 + 8, :] * k_j)
                    for hd, hs in enumerate(heads):
                        s = jnp.sum(e[:, hs], axis=-1, keepdims=True)
                        acc[half][hd] = jnp.where(col8 == base + j, s, acc[half][hd])
            for half in range(HG_SUB // 8):
                for hd in range(H_HEADS):
                    diag[hd][base // 8 + half] = acc[half][hd]

        g_last = g2[CHUNK - 1:CHUNK, :]
        qg = q * jnp.exp2(g2)
        kdec = k * jnp.exp2(g_last - g2)
        e_last = jnp.exp2(g_last)
        attn = []
        for hd, hs in enumerate(heads):
            g_h, q_h, k_h = g2[:, hs], q[:, hs], k[:, hs]
            qcat, kcat = [], []
            for sub in range(1, n_sub):
                base = sub * HG_SUB
                g_r = g_h[base:base + 1, :]
                q_sub = q_h[base:base + HG_SUB] * jnp.exp2(jnp.minimum(g_h[base:base + HG_SUB] - g_r, 0.0))
                pieces = [jnp.zeros((base, H_HEAD_DIM), F32), q_sub]
                if base + HG_SUB < CHUNK:
                    pieces.append(jnp.zeros((CHUNK - base - HG_SUB, H_HEAD_DIM), F32))
                qcat.append(jnp.concatenate(pieces, axis=0))
                k_sub = k_h[0:base] * jnp.exp2(jnp.minimum(g_r - g_h[0:base], 0.0))
                kcat.append(jnp.concatenate([k_sub, jnp.zeros((CHUNK - base, H_HEAD_DIM), F32)], axis=0))
            a_off = _dot(jnp.concatenate(qcat, axis=1), jnp.concatenate(kcat, axis=1), NT)
            a_diag = jnp.concatenate(diag[hd], axis=0)[:, 0:CHUNK]
            attn.append(jnp.where(off_mask, a_off, jnp.where(causal, a_diag, 0.0)))
        sts = [st_ref[hd] for hd in range(H_HEADS)]
        inter = [_dot(qg[:, hs], sts[hd], NT) for hd, hs in enumerate(heads)]
        upd = [_dot(v[:, hs], kdec[:, hs], TN) for hs in heads]
        intra = [_dot(attn[hd], v[:, hs]) for hd, hs in enumerate(heads)]
        outs = []
        for hd, hs in enumerate(heads):
            st_ref[hd] = sts[hd] * e_last[:, hs] + upd[hd]
            o_h = (intra[hd] + inter[hd]) * _sigmoid(g_ref[rows, hs].astype(F32))
            outs.append(o_h * lax.rsqrt(jnp.mean(o_h * o_h, axis=-1, keepdims=True) + NORM_EPS))
        o_ref[rows, :] = (jnp.concatenate(outs, axis=-1) * nw_ref[...]).astype(o_ref.dtype)
        return carry

    lax.fori_loop(0, rb // CHUNK, chunk, 0)


def _hgrn(p, lb_logits, norm_w, layer, batch, lp):
    rb = SEQ_BLOCK
    nb = lp // rb
    t = p.shape[0]
    cb = C_H // H_WIDTH

    def col(k):
        return pl.BlockSpec((rb, H_WIDTH), lambda b, j, k=k: (b * nb + j, cb + k))

    return pl.pallas_call(
        functools.partial(_hgrn_kernel, layer=layer, rb=rb),
        grid=(batch, nb),
        in_specs=[col(0), col(1), col(2), col(3),
                  pl.BlockSpec((DEPTH, H_WIDTH), lambda b, j: (0, 0)),
                  pl.BlockSpec((1, H_WIDTH), lambda b, j: (0, 0))],
        out_specs=pl.BlockSpec((rb, H_WIDTH), lambda b, j: (b * nb + j, 0)),
        out_shape=jax.ShapeDtypeStruct((t, H_WIDTH), BF16),
        scratch_shapes=[pltpu.VMEM((H_HEADS, H_HEAD_DIM, H_HEAD_DIM), F32)]
        + [pltpu.VMEM((CHUNK, H_WIDTH), F32) for _ in range(3)],
        compiler_params=_params(("parallel", "arbitrary")),
        name="hgrn2",
    )(p, p, p, p, lb_logits, norm_w)


HIST = 8


def _mamba_kernel(z_ref, x_ref, bc_ref, dt_ref, cwx_ref, cwb_ref, cbx_ref, cbb_ref,
                  dtb_ref, alog_ref, dsk_ref, nw_ref, o_ref,
                  st_ref, xe, be, xa, ba, *, rb):
    jb = pl.program_id(1)

    @pl.when(jb == 0)
    def _():
        st_ref[...] = jnp.zeros(st_ref.shape, F32)
        xe[0:HIST, :] = jnp.zeros((HIST, M_WIDTH), F32)
        be[0:HIST, :] = jnp.zeros((HIST, 2 * M_BC), F32)

    def conv(ext, src_ref, w_ref, b_ref, dst):
        ext[HIST:HIST + rb, :] = src_ref[...].astype(F32)
        acc = b_ref[...]
        for tap in range(M_CONV):
            o = HIST - (M_CONV - 1) + tap
            acc = acc + w_ref[tap:tap + 1, :] * ext[o:o + rb, :]
        dst[...] = _silu(acc)
        ext[0:HIST, :] = ext[rb:rb + HIST, :]

    conv(xe, x_ref, cwx_ref, cbx_ref, xa)
    conv(be, bc_ref, cwb_ref, cbb_ref, ba)

    tri = _tri(CHUNK)
    ones = jnp.ones((CHUNK, CHUNK), F32)
    row = _iota((CHUNK, 1), 0)
    expand = (_iota((LANES, M_WIDTH), 1) // M_HEAD_DIM == _iota((LANES, M_WIDTH), 0)).astype(F32)
    pos_j = _iota((CHUNK, M_WIDTH), 1) % CHUNK
    row_i = _iota((CHUNK, M_WIDTH), 0)
    a_neg = -jnp.exp(alog_ref[...])
    hpg = M_HEADS // M_GROUPS

    def chunk(c, carry):
        r0 = pl.multiple_of(c * CHUNK, CHUNK)
        rows = pl.ds(r0, CHUNK)
        valid = ((jb * rb + r0 + row) >= N_PAD).astype(F32)
        xs = xa[rows, :]
        bcm = ba[rows, :]
        dt = _softplus(dt_ref[rows, :].astype(F32) + dtb_ref[...]) * valid
        dt_e = _mask_dot_lhs(dt, expand)
        da_e = _mask_dot_lhs(dt * a_neg, expand)
        acum = _mask_dot_rhs(tri, da_e)
        acum_j = _mask_dot_rhs(ones, jnp.where(row_i <= pos_j, da_e, 0.0))
        lmat = jnp.where(row_i >= pos_j, jnp.exp(jnp.minimum(acum - acum_j, 0.0)), 0.0)
        a_last = acum[CHUNK - 1:CHUNK, :]
        e_cum = jnp.exp(acum)
        e_end = jnp.exp(a_last - acum)
        e_last = jnp.exp(a_last)
        xc = xs * dt_e
        scores = []
        for g in range(M_GROUPS):
            b_g = bcm[:, g * M_STATE:(g + 1) * M_STATE]
            c_g = bcm[:, M_BC + g * M_STATE:M_BC + (g + 1) * M_STATE]
            scores.append(_dot(c_g, jnp.concatenate([b_g] * hpg, axis=0), NT))
        attn = jnp.concatenate(scores, axis=1) * lmat
        ys = []
        for pr in range(M_HEADS // 2):
            ps = slice(pr * LANES, (pr + 1) * LANES)
            g = (2 * pr) // hpg
            b_g = bcm[:, g * M_STATE:(g + 1) * M_STATE]
            c_g = bcm[:, M_BC + g * M_STATE:M_BC + (g + 1) * M_STATE]
            xc_p = xc[:, ps]
            st = st_ref[pr]
            y = _dot(attn[:, ps], _block_diag(xc_p)) + _dot(c_g, st) * e_cum[:, ps]
            st_ref[pr] = st * e_last[:, ps] + _dot(b_g, xc_p * e_end[:, ps], TN)
            ys.append(y)
        y = (jnp.concatenate(ys, axis=1) + dsk_ref[...] * xs) * _silu(z_ref[rows, :].astype(F32))
        gw = M_WIDTH // M_GROUPS
        outs = []
        for g in range(M_GROUPS):
            yg = y[:, g * gw:(g + 1) * gw]
            outs.append(yg * lax.rsqrt(jnp.mean(yg * yg, axis=-1, keepdims=True) + NORM_EPS))
        o_ref[rows, :] = (jnp.concatenate(outs, axis=1) * nw_ref[...]).astype(o_ref.dtype)
        return carry

    lax.fori_loop(0, rb // CHUNK, chunk, 0)


def _mamba(p, conv_w, conv_b, dt_bias, a_log, d_skip, norm_w, batch, lp):
    rb = SEQ_BLOCK
    nb = lp // rb
    t = p.shape[0]

    def col(off, width):
        return pl.BlockSpec((rb, width), lambda b, j: (b * nb + j, off // width))

    def full(shape):
        return pl.BlockSpec(shape, lambda b, j: (0,) * len(shape))

    pad = LANES - M_HEADS
    dtb = jnp.pad(dt_bias.reshape(1, M_HEADS), ((0, 0), (0, pad)))
    alog = jnp.pad(a_log.reshape(1, M_HEADS), ((0, 0), (0, pad)))
    dsk = jnp.repeat(d_skip, M_HEAD_DIM).reshape(1, M_WIDTH)
    return pl.pallas_call(
        functools.partial(_mamba_kernel, rb=rb),
        grid=(batch, nb),
        in_specs=[col(C_Z, M_WIDTH), col(C_X, M_WIDTH), col(C_BC, 2 * M_BC), col(C_DT, LANES),
                  full((M_CONV, M_WIDTH)), full((M_CONV, 2 * M_BC)),
                  full((1, M_WIDTH)), full((1, 2 * M_BC)),
                  full((1, LANES)), full((1, LANES)), full((1, M_WIDTH)), full((1, M_WIDTH))],
        out_specs=pl.BlockSpec((rb, M_WIDTH), lambda b, j: (b * nb + j, 0)),
        out_shape=jax.ShapeDtypeStruct((t, M_WIDTH), BF16),
        scratch_shapes=[pltpu.VMEM((M_HEADS // 2, M_STATE, LANES), F32),
                        pltpu.VMEM((rb + HIST, M_WIDTH), F32),
                        pltpu.VMEM((rb + HIST, 2 * M_BC), F32),
                        pltpu.VMEM((rb, M_WIDTH), F32),
                        pltpu.VMEM((rb, 2 * M_BC), F32)],
        compiler_params=_params(("parallel", "arbitrary")),
        name="mamba2",
    )(p, p, p, p, conv_w[:, :M_WIDTH], conv_w[:, M_WIDTH:], conv_b[None, :M_WIDTH],
      conv_b[None, M_WIDTH:], dtb, alog, dsk, norm_w[None, :])


R_SHIFT_W = 3 * R_WIDTH
R_PRE = 2


def _rwkv_kernel(*refs, first, rb):
    if first:
        (rkv_ref, g1_ref, wa_ref, mu_rkv, mu_g1, mu_wa,
         w0_ref, w2_ref, a0_ref, a2_ref, g2_ref, kk_ref, ka_ref, rk_ref, gnw_ref, gnb_ref,
         o_ref, vf_out,
         st_ref, e_rkv, e_g1, e_wa, dec_s,
         lw_s, r_s, k_s, v_s, a_s, b_s, o_s, g_s, rt_s, wt_s, uv_s, arb_s, ov_s, bh_s, kh_s) = refs
    else:
        (rkv_ref, g1_ref, wa_ref, vr_ref, vf_ref, mu_rkv, mu_g1, mu_wa, mu_vr,
         w0_ref, w2_ref, a0_ref, a2_ref, g2_ref, kk_ref, ka_ref, rk_ref, gnw_ref, gnb_ref,
         v0_ref, v2_ref,
         o_ref,
         st_ref, e_rkv, e_g1, e_wa, e_vr, dec_s,
         lw_s, r_s, k_s, v_s, a_s, b_s, o_s, g_s, rt_s, wt_s, uv_s, arb_s, ov_s, bh_s, kh_s) = refs
    jb = pl.program_id(1)
    exts = [e_rkv, e_g1, e_wa] + ([] if first else [e_vr])

    @pl.when(jb == 0)
    def _():
        st_ref[...] = jnp.zeros(st_ref.shape, F32)
        for ext in exts:
            ext[0:HIST, :] = jnp.zeros((HIST, ext.shape[1]), F32)

    def shift(ext, src_ref, mu_ref):
        ext[HIST:HIST + rb, :] = src_ref[...].astype(F32)
        cur = ext[HIST:HIST + rb, :]
        prev = ext[HIST - 1:HIST - 1 + rb, :]
        out = cur + (prev - cur) * mu_ref[...]
        ext[0:HIST, :] = ext[rb:rb + HIST, :]
        return out

    rkv = shift(e_rkv, rkv_ref, mu_rkv)
    gl = shift(e_g1, g1_ref, mu_g1)
    wa = shift(e_wa, wa_ref, mu_wa)
    r = rkv[:, 0:R_WIDTH]
    k = rkv[:, R_WIDTH:2 * R_WIDTH]
    v = rkv[:, 2 * R_WIDTH:3 * R_WIDTH]
    wl = wa[:, 0:LANES]
    al = wa[:, LANES:2 * LANES]

    valid = ((jb * rb + _iota((rb, 1), 0)) >= N_PAD).astype(F32)
    w_log = -_softplus(-(w0_ref[...] + _dot(jnp.tanh(wl), w2_ref[...]))) - 0.5
    lw_s[...] = -jnp.exp(w_log)
    a = _sigmoid(a0_ref[...] + _dot(al, a2_ref[...]))
    if first:
        vf_out[...] = v
    else:
        vl = shift(e_vr, vr_ref, mu_vr)
        v = v + (vf_ref[...] - v) * _sigmoid(v0_ref[...] + _dot(vl, v2_ref[...]))
    g_s[...] = _dot(_sigmoid(gl), g2_ref[...])

    seg_ones = (_iota((LANES, LANES), 0) // R_HEAD_DIM == _iota((LANES, LANES), 1) // R_HEAD_DIM).astype(F32)

    def head_sum(x):
        return jnp.concatenate(
            [_mask_dot_lhs(x[:, s * LANES:(s + 1) * LANES], seg_ones, terms=2) for s in range(N_PAIRS)], axis=1)

    kk = k * kk_ref[...]
    kk = kk * lax.rsqrt(jnp.maximum(head_sum(kk * kk), L2_EPS))
    kh = k * (1.0 + (a - 1.0) * ka_ref[...]) * valid
    r_s[...] = r
    k_s[...] = kh
    v_s[...] = v
    a_s[...] = -kk
    b_s[...] = kk * a

    tri = _tri(CHUNK)
    t_i = _iota((CHUNK, LANES), 0)
    s_j = _iota((CHUNK, LANES), 1) % CHUNK
    strict = t_i > s_j
    incl = t_i >= s_j
    eye = (t_i == s_j).astype(F32)
    same_head = _iota((LANES, LANES), 0) // R_HEAD_DIM == _iota((LANES, LANES), 1) // R_HEAD_DIM
    n_double = 5
    assert 2 ** (n_double + 1) == CHUNK

    pairs = [slice(pr * LANES, (pr + 1) * LANES) for pr in range(N_PAIRS)]

    def precompute(cc, carry):
        units = []
        for sub in range(R_PRE):
            c = cc * R_PRE + sub
            rows = pl.ds(pl.multiple_of(c * CHUNK, CHUNK), CHUNK)
            lw = lw_s[rows, :]
            cum = _mask_dot_rhs(tri, lw)
            c_last = cum[CHUNK - 1:CHUNK, :]
            e_inv = jnp.exp(-cum)
            e_end = jnp.exp(c_last - cum)
            b_c, k_c, v_c = b_s[rows, :], k_s[rows, :], v_s[rows, :]
            a_t = a_s[rows, :] * jnp.exp(cum - lw)
            r_t = r_s[rows, :] * jnp.exp(cum)
            b_t = b_c * e_inv
            k_t = k_c * e_inv
            bh_s[rows, :] = b_c * e_end
            kh_s[rows, :] = k_c * e_end
            rt_s[rows, :] = r_t
            dec_s[pl.ds(c, 1), :] = jnp.exp(c_last)
            units += [(rows, ps, a_t[:, ps], r_t[:, ps], b_t[:, ps], k_t[:, ps], v_c[:, ps]) for ps in pairs]
        m = [_dot(jnp.concatenate([a_t, r_t], axis=0),
                  jnp.concatenate([_block_diag(b_t), _block_diag(k_t)], axis=0), NT)
             for (_, _, a_t, r_t, b_t, k_t, _) in units]
        a_ab = [jnp.where(strict, x[0:CHUNK, 0:LANES], 0.0) for x in m]
        a_ak = [jnp.where(strict, x[0:CHUNK, LANES:2 * LANES], 0.0) for x in m]
        a_rk = [jnp.where(incl, x[CHUNK:2 * CHUNK, LANES:2 * LANES], 0.0) for x in m]
        for x, un in zip(m, units):
            arb_s[un[0], un[1]] = jnp.where(incl, x[CHUNK:2 * CHUNK, 0:LANES], 0.0)
        inv = [eye + x for x in a_ab]
        pw = [_dot(x, _block_diag(x)) for x in a_ab]
        for it in range(n_double - 1):
            both = [_dot(jnp.concatenate([p_, i_], axis=0), _block_diag(p_)) for p_, i_ in zip(pw, inv)]
            pw = [x[0:CHUNK] for x in both]
            inv = [i_ + x[CHUNK:2 * CHUNK] for i_, x in zip(inv, both)]
        inv = [i_ + _dot(i_, _block_diag(p_)) for i_, p_ in zip(inv, pw)]
        vb = [_block_diag(un[6]) for un in units]
        wt = [_dot(i_, _block_diag(un[2])) for i_, un in zip(inv, units)]
        t1 = [_dot(x, b_) for x, b_ in zip(a_ak, vb)]
        ov = [_dot(x, b_) for x, b_ in zip(a_rk, vb)]
        uv = [_dot(i_, _block_diag(x)) for i_, x in zip(inv, t1)]
        for i, un in enumerate(units):
            wt_s[un[0], un[1]] = wt[i]
            uv_s[un[0], un[1]] = uv[i]
            ov_s[un[0], un[1]] = ov[i]
        return carry

    n_chunks = rb // CHUNK
    assert n_chunks % R_PRE == 0
    lax.fori_loop(0, n_chunks // R_PRE, precompute, 0)

    sts = [st_ref[pr] for pr in range(N_PAIRS)]
    for c in range(n_chunks):
        rows = slice(c * CHUNK, (c + 1) * CHUNK)
        dec = dec_s[c:c + 1, :]
        u = [_dot(wt_s[rows, ps], st, NT) + uv_s[rows, ps] for ps, st in zip(pairs, sts)]
        upd = [_dot(jnp.concatenate([u_, v_s[rows, ps]], axis=0),
                    jnp.concatenate([bh_s[rows, ps], kh_s[rows, ps]], axis=0), TN)
               for ps, u_ in zip(pairs, u)]
        new_sts = [st * dec[:, ps] + jnp.where(same_head, x, 0.0) for st, ps, x in zip(sts, pairs, upd)]
        o1 = [_dot(rt_s[rows, ps], st, NT) for ps, st in zip(pairs, sts)]
        o2 = [_dot(arb_s[rows, ps], _block_diag(u_)) for ps, u_ in zip(pairs, u)]
        for pr, ps in enumerate(pairs):
            o_s[rows, ps] = o1[pr] + o2[pr] + ov_s[rows, ps]
        sts = new_sts
    for pr in range(N_PAIRS):
        st_ref[pr] = sts[pr]

    o = o_s[...]
    inv_n = 1.0 / R_HEAD_DIM
    mu = head_sum(o) * inv_n
    d = o - mu
    var = head_sum(d * d) * inv_n
    o = d * lax.rsqrt(var + R_GN_EPS) * gnw_ref[...] + gnb_ref[...]
    o = o + head_sum(r_s[...] * k_s[...] * rk_ref[...]) * v_s[...]
    o_ref[...] = (o * g_s[...]).astype(o_ref.dtype)


def _rwkv(p, v_first, prm, first, batch, lp):
    rb = SEQ_BLOCK
    nb = lp // rb
    t = p.shape[0]

    def col(off, width):
        return pl.BlockSpec((rb, width), lambda b, j: (b * nb + j, off // width))

    def full(a):
        return pl.BlockSpec(a.shape, lambda b, j: (0,) * a.ndim)

    row_spec = pl.BlockSpec((rb, R_WIDTH), lambda b, j: (b * nb + j, 0))
    acts = [p, p, p]
    act_specs = [col(C_RKV, R_SHIFT_W), col(C_G1, R_GATE_RANK), col(C_WA, 2 * LANES)]
    mus = [prm["mu_rkv"], prm["mu_g1"], prm["mu_wa"]]
    tail = []
    if not first:
        acts += [p, v_first]
        act_specs += [col(C_VRES, LANES), row_spec]
        mus.append(prm["mu_vr"])
        tail = [prm["v0"], prm["v2"]]
    consts = mus + [prm[n] for n in ("w0", "w2", "a0", "a2", "g2", "k_k", "k_a", "r_k", "gn_w", "gn_b")] + tail
    out_shape = [jax.ShapeDtypeStruct((t, R_WIDTH), BF16)]
    out_specs = [row_spec]
    if first:
        out_shape.append(jax.ShapeDtypeStruct((t, R_WIDTH), F32))
        out_specs.append(row_spec)
    ext_w = [R_SHIFT_W, R_GATE_RANK, 2 * LANES] + ([] if first else [LANES])
    scratch = ([pltpu.VMEM((N_PAIRS, LANES, LANES), F32)]
               + [pltpu.VMEM((rb + HIST, w), F32) for w in ext_w]
               + [pltpu.VMEM((max(rb // CHUNK, 8), R_WIDTH), F32)]
               + [pltpu.VMEM((rb, R_WIDTH), F32) for _ in range(15)])
    return pl.pallas_call(
        functools.partial(_rwkv_kernel, first=first, rb=rb),
        grid=(batch, nb),
        in_specs=act_specs + [full(a) for a in consts],
        out_specs=out_specs,
        out_shape=out_shape,
        scratch_shapes=scratch,
        compiler_params=_params(("parallel", "arbitrary")),
        name="rwkv7",
    )(*acts, *consts)


def _merge_kernel(oh_ref, om_ref, or_ref, g0_ref, g1_ref, g2_ref, wh_ref, wm_ref, wr_ref, o_ref):
    def gated(g_ref, x_ref, w_ref):
        return _sigmoid(g_ref[...].astype(F32)) * jnp.dot(x_ref[...], w_ref[...], preferred_element_type=F32)

    acc = gated(g0_ref, oh_ref, wh_ref) + gated(g1_ref, om_ref, wm_ref) + gated(g2_ref, or_ref, wr_ref)
    o_ref[...] = acc.astype(o_ref.dtype)


def _merge(o_h, o_m, o_r, p, w_h, w_m, w_r, tm, tn):
    t = p.shape[0]
    gb = C_GATE // tn
    nt = D_MODEL // tn

    def act():
        return pl.BlockSpec((tm, H_WIDTH), lambda i, j: (i, 0))

    def gate(k):
        return pl.BlockSpec((tm, tn), lambda i, j, k=k: (i, gb + k * nt + j))

    def wt():
        return pl.BlockSpec((H_WIDTH, tn), lambda i, j: (0, j))

    return pl.pallas_call(
        _merge_kernel,
        grid=(t // tm, nt),
        in_specs=[act(), act(), act(), gate(0), gate(1), gate(2), wt(), wt(), wt()],
        out_specs=pl.BlockSpec((tm, tn), lambda i, j: (i, j)),
        out_shape=jax.ShapeDtypeStruct((t, D_MODEL), BF16),
        compiler_params=_params(("parallel", "arbitrary")),
        name="merge",
    )(o_h, o_m, o_r, p, p, p, w_h, w_m, w_r)


def _row_valid(tm, lp):
    pos = (pl.program_id(0) * tm + _iota((tm, 1), 0)) % lp
    return (pos >= N_PAD).astype(F32)


def _tail_kernel(h_ref, m_ref, wo_ref, lnw_ref, w1_ref, w2_ref, lnf_ref, o_ref, u_ref, *, tm, lp, final):
    kk = pl.program_id(1)

    @pl.when(kk == 0)
    def _():
        valid = _row_valid(tm, lp)
        mixed = jnp.where(valid > 0.0, m_ref[...], jnp.zeros((), m_ref.dtype))
        x = h_ref[...] + jnp.dot(mixed, wo_ref[...], preferred_element_type=F32)
        ms = jnp.mean(x * x, axis=-1, keepdims=True)
        u_ref[...] = (x * lax.rsqrt(ms + NORM_EPS) * lnw_ref[...] * valid).astype(BF16)
        o_ref[...] = x

    a = jnp.maximum(jnp.dot(u_ref[...], w1_ref[...], preferred_element_type=F32), 0.0)
    o_ref[...] += jnp.dot((a * a).astype(BF16), w2_ref[...], preferred_element_type=F32)

    if final:
        @pl.when(kk == pl.num_programs(1) - 1)
        def _():
            y = o_ref[...]
            ms = jnp.mean(y * y, axis=-1, keepdims=True)
            o_ref[...] = y * lax.rsqrt(ms + NORM_EPS) * lnf_ref[...]


def _tail(h, mixed, w_out, lnw, w1, w2, lnf, tm, th, lp, final):
    t = h.shape[0]
    return pl.pallas_call(
        functools.partial(_tail_kernel, tm=tm, lp=lp, final=final),
        grid=(t // tm, MLP_HIDDEN // th),
        in_specs=[pl.BlockSpec((tm, D_MODEL), lambda i, k: (i, 0)),
                  pl.BlockSpec((tm, D_MODEL), lambda i, k: (i, 0)),
                  pl.BlockSpec((D_MODEL, D_MODEL), lambda i, k: (0, 0), pipeline_mode=pl.Buffered(1)),
                  pl.BlockSpec((1, D_MODEL), lambda i, k: (0, 0)),
                  pl.BlockSpec((D_MODEL, th), lambda i, k: (0, k)),
                  pl.BlockSpec((th, D_MODEL), lambda i, k: (k, 0)),
                  pl.BlockSpec((1, D_MODEL), lambda i, k: (0, 0))],
        out_specs=pl.BlockSpec((tm, D_MODEL), lambda i, k: (i, 0)),
        out_shape=jax.ShapeDtypeStruct((t, D_MODEL), F32),
        scratch_shapes=[pltpu.VMEM((tm, D_MODEL), BF16)],
        compiler_params=_params(("parallel", "arbitrary")),
        name="out_mlp",
    )(h, mixed, w_out, lnw, w1, w2, lnf)


def _pad_cols(a, width):
    return jnp.pad(a, ((0, 0), (0, width - a.shape[1])))


def _pad_rows(a, height):
    return jnp.pad(a, ((0, height - a.shape[0]), (0, 0)))


def _split_in_cols(a, a_vres):
    o = 0
    hcols = a[:, o:o + H_COLS]; o += H_COLS
    z = a[:, o:o + M_WIDTH]; o += M_WIDTH
    xm = a[:, o:o + M_WIDTH]; o += M_WIDTH
    bc = a[:, o:o + 2 * M_BC]; o += 2 * M_BC
    dt = a[:, o:o + M_HEADS]; o += M_HEADS
    rkv = a[:, o:o + 3 * R_WIDTH]; o += 3 * R_WIDTH
    w1 = a[:, o:o + R_DECAY_RANK]; o += R_DECAY_RANK
    a1 = a[:, o:o + R_AAA_RANK]; o += R_AAA_RANK
    g1 = a[:, o:o + R_GATE_RANK]; o += R_GATE_RANK
    gates = a[:, o:o + GATE_COLS]; o += GATE_COLS
    assert o == IN_COLS
    out = jnp.concatenate([hcols, gates, z, xm, rkv, bc, g1, _pad_cols(w1, LANES), _pad_cols(a1, LANES),
                           _pad_cols(dt, LANES), _pad_cols(a_vres, LANES)], axis=1)
    assert out.shape[1] == P_COLS
    return out


def _tile(t, candidates):
    for c in candidates:
        if t % c == 0:
            return c
    raise ValueError(f"no tile for {t}")


def kernel(x, meta, ln1_w, ln2_w, lnf_w, w_in, w_in_vres, hg_lb_logits, hg_norm_w, m_conv_w, m_conv_b, m_dt_bias, m_a_log, m_d, m_norm_w, r_mu, r_mu_vres, r_w0, r_w2, r_a0, r_a2, r_v0, r_v2, r_g2, r_k_k, r_k_a, r_r_k, r_gn_w, r_gn_b, w_up_h, w_up_m, w_up_r, w_out, w_mlp_in, w_mlp_out):
    batch, seq, d = x.shape
    assert d == D_MODEL and M_HEAD_DIM == CHUNK and R_HEAD_DIM == CHUNK
    l_real = seq + CHUNK
    lp = -(-l_real // ROW_ALIGN) * ROW_ALIGN
    t = batch * lp
    tm_proj = _tile(t, (1280, 640))
    tm = _tile(t, (640,))
    tm_tail = _tile(t, (512, 256))

    h = jnp.concatenate([jnp.zeros((batch, N_PAD, d), x.dtype),
                         jnp.broadcast_to(meta.astype(x.dtype), (batch, N_META, d)),
                         x, jnp.zeros((batch, lp - l_real, d), x.dtype)], axis=1).reshape(t, d)

    v_first = None
    for l in range(DEPTH):
        vres_w = w_in_vres[l - 1] if l > 0 else jnp.zeros((d, R_MV_RANK), w_in.dtype)
        w_comb = _split_in_cols(w_in[l], vres_w).astype(BF16)
        p = _proj(h, ln1_w[l][None, :], w_comb, tm_proj, 1280)

        o_h = _hgrn(p, hg_lb_logits, hg_norm_w[l].reshape(1, H_WIDTH), l, batch, lp)
        o_m = _mamba(p, m_conv_w[l], m_conv_b[l], m_dt_bias[l], m_a_log[l], m_d[l], m_norm_w[l], batch, lp)

        mu = r_mu[l][None, :]
        o = 3 * R_WIDTH
        prm = {
            "mu_rkv": mu[:, :o],
            "mu_wa": jnp.concatenate([_pad_cols(mu[:, o:o + R_DECAY_RANK], LANES),
                                      _pad_cols(mu[:, o + R_DECAY_RANK:o + R_DECAY_RANK + R_AAA_RANK], LANES)], axis=1),
            "mu_g1": mu[:, o + R_DECAY_RANK + R_AAA_RANK:],
            "w0": r_w0[l][None, :], "w2": _pad_rows(r_w2[l], LANES).astype(BF16),
            "a0": r_a0[l][None, :], "a2": _pad_rows(r_a2[l], LANES).astype(BF16),
            "g2": r_g2[l].astype(BF16),
            "k_k": r_k_k[l][None, :], "k_a": r_k_a[l][None, :], "r_k": r_r_k[l].reshape(1, R_WIDTH),
            "gn_w": r_gn_w[l][None, :], "gn_b": r_gn_b[l][None, :],
        }
        if l == 0:
            o_r, v_first = _rwkv(p, None, prm, True, batch, lp)
        else:
            prm["mu_vr"] = _pad_cols(r_mu_vres[l - 1][None, :], LANES)
            prm["v0"] = r_v0[l - 1][None, :]
            prm["v2"] = _pad_rows(r_v2[l - 1], LANES).astype(BF16)
            (o_r,) = _rwkv(p, v_first, prm, False, batch, lp)

        mixed = _merge(o_h, o_m, o_r, p, w_up_h[l].astype(BF16), w_up_m[l].astype(BF16),
                       w_up_r[l].astype(BF16), tm, 1024)
        h = _tail(h, mixed, w_out[l].astype(BF16), ln2_w[l][None, :], w_mlp_in[l].astype(BF16),
                  w_mlp_out[l].astype(BF16), lnf_w[None, :], tm_tail, 1024, lp, l == DEPTH - 1)

    return h.reshape(batch, lp, d)[:, CHUNK:l_real]
```

```python
import functools

import jax
import jax.numpy as jnp
from jax import lax
from jax.experimental import pallas as pl
from jax.experimental.pallas import tpu as pltpu

F32 = jnp.float32
BF16 = jnp.bfloat16

LANES = 128

D_MODEL = 2048
DEPTH = 2
N_META = 16
CHUNK = 64
N_PAD = CHUNK - N_META
MLP_HIDDEN = 4 * D_MODEL
NORM_EPS = 1e-6
N_BRANCH = 3
L2_EPS = 1e-24

H_WIDTH = D_MODEL // 2
H_HEAD_DIM = 128
H_HEADS = H_WIDTH // H_HEAD_DIM

M_WIDTH = D_MODEL // 2
M_HEAD_DIM = 64
M_HEADS = M_WIDTH // M_HEAD_DIM
M_GROUPS = 2
M_STATE = 128
M_CONV = 4
M_BC = M_GROUPS * M_STATE

R_WIDTH = D_MODEL // 2
R_HEAD_DIM = 64
R_HEADS = R_WIDTH // R_HEAD_DIM
R_DECAY_RANK = max(32, int(round(1.8 * D_MODEL ** 0.5 / 32)) * 32)
R_AAA_RANK = max(32, int(round(1.8 * D_MODEL ** 0.5 / 32)) * 32)
R_MV_RANK = max(32, int(round(1.3 * D_MODEL ** 0.5 / 32)) * 32)
R_GATE_RANK = max(32, int(round(0.6 * D_MODEL ** 0.8 / 32)) * 32)
R_GN_EPS = 64e-5
N_PAIRS = R_HEADS // 2

H_COLS = 4 * H_WIDTH
M_CONV_CH = M_WIDTH + 2 * M_BC
M_COLS = M_WIDTH + M_CONV_CH + M_HEADS
R_COLS = 3 * R_WIDTH + R_DECAY_RANK + R_AAA_RANK + R_GATE_RANK
GATE_COLS = N_BRANCH * D_MODEL
IN_COLS = H_COLS + M_COLS + R_COLS + GATE_COLS

C_H = 0
C_GATE = C_H + H_COLS
C_Z = C_GATE + GATE_COLS
C_X = C_Z + M_WIDTH
C_RKV = C_X + M_WIDTH
C_BC = C_RKV + 3 * R_WIDTH
C_G1 = C_BC + 2 * M_BC
C_WA = C_G1 + R_GATE_RANK
C_DT = C_WA + 2 * LANES
C_VRES = C_DT + LANES
P_COLS = C_VRES + LANES

ROW_ALIGN = 1280
SEQ_BLOCK = 256
VMEM_LIMIT = 56 * 1024 * 1024

NN = (((1,), (0,)), ((), ()))
NT = (((1,), (1,)), ((), ()))
TN = (((0,), (0,)), ((), ()))


def _dot(a, b, dims=NN):
    return lax.dot_general(a.astype(BF16), b.astype(BF16), dims, preferred_element_type=F32)


def _split(x, terms):
    out = []
    for _ in range(terms - 1):
        hi = x.astype(BF16)
        out.append(hi)
        x = x - hi.astype(F32)
    out.append(x.astype(BF16))
    return out


def _mask_dot_rhs(m, x, terms=3):
    mb = m.astype(BF16)
    return jnp.dot(jnp.concatenate([mb] * terms, axis=1), jnp.concatenate(_split(x, terms), axis=0),
                   preferred_element_type=F32)


def _mask_dot_lhs(x, m, terms=3):
    mb = m.astype(BF16)
    return jnp.dot(jnp.concatenate(_split(x, terms), axis=1), jnp.concatenate([mb] * terms, axis=0),
                   preferred_element_type=F32)


def _sigmoid(x):
    return 1.0 / (1.0 + jnp.exp(-x))


def _silu(x):
    return x * _sigmoid(x)


def _softplus(x):
    return jnp.maximum(x, 0.0) + jnp.log(1.0 + jnp.exp(-jnp.abs(x)))


def _iota(shape, dim):
    return lax.broadcasted_iota(jnp.int32, shape, dim)


def _tri(n):
    return (_iota((n, n), 0) >= _iota((n, n), 1)).astype(F32)


def _chunk_tri(n, lower=True):
    r, c = _iota((n, n), 0), _iota((n, n), 1)
    same = r // CHUNK == c // CHUNK
    return (same & (r >= c) if lower else same).astype(F32)


def _block_diag(x):
    lane = _iota(x.shape, 1)
    return jnp.concatenate([jnp.where(lane < 64, x, 0.0), jnp.where(lane >= 64, x, 0.0)], axis=0)


def _params(sem):
    return pltpu.CompilerParams(dimension_semantics=sem, vmem_limit_bytes=VMEM_LIMIT)


def _proj_kernel(h_ref, lnw_ref, w_ref, o_ref, u_ref):
    @pl.when(pl.program_id(1) == 0)
    def _():
        x = h_ref[...]
        ms = jnp.mean(x * x, axis=-1, keepdims=True)
        u_ref[...] = (x * lax.rsqrt(ms + NORM_EPS) * lnw_ref[...]).astype(BF16)

    o_ref[...] = jnp.dot(u_ref[...], w_ref[...], preferred_element_type=F32).astype(o_ref.dtype)


def _proj(h, lnw, w, tm, tn):
    t = h.shape[0]
    return pl.pallas_call(
        _proj_kernel,
        grid=(t // tm, P_COLS // tn),
        in_specs=[
            pl.BlockSpec((tm, D_MODEL), lambda i, j: (i, 0)),
            pl.BlockSpec((1, D_MODEL), lambda i, j: (0, 0)),
            pl.BlockSpec((D_MODEL, tn), lambda i, j: (0, j)),
        ],
        out_specs=pl.BlockSpec((tm, tn), lambda i, j: (i, j)),
        out_shape=jax.ShapeDtypeStruct((t, P_COLS), BF16),
        scratch_shapes=[pltpu.VMEM((tm, D_MODEL), BF16)],
        compiler_params=_params(("parallel", "arbitrary")),
        name="in_proj",
    )(h, lnw, w)


HG_SUB = 8
LOG2E = 1.4426950408889634


def _hgrn_kernel(q_ref, f_ref, i_ref, g_ref, lbl_ref, nw_ref, o_ref,
                 st_ref, qs, ks, gs, *, layer, rb):
    jb = pl.program_id(1)

    @pl.when(jb == 0)
    def _():
        st_ref[...] = jnp.zeros(st_ref.shape, F32)

    lg = lbl_ref[...]
    mx = jnp.max(lg, axis=0, keepdims=True)
    ex = jnp.exp(lg - mx)
    sm = ex / jnp.sum(ex, axis=0, keepdims=True)
    lb = jnp.sum(sm[0:layer + 1], axis=0, keepdims=True) - sm[0:1]

    assert HG_SUB == 8
    col8 = _iota((HG_SUB, H_HEAD_DIM), 1)
    key_row = [(_iota((HG_SUB, H_HEAD_DIM), 0) + (HG_SUB - d)) % HG_SUB for d in range(HG_SUB)]
    heads = [slice(hd * H_HEAD_DIM, (hd + 1) * H_HEAD_DIM) for hd in range(H_HEADS)]
    arow = _iota((CHUNK, CHUNK), 0)
    acol = _iota((CHUNK, CHUNK), 1)
    off_mask = acol < (arow // HG_SUB) * HG_SUB
    causal = arow >= acol
    n_sub = CHUNK // HG_SUB

    tri = _tri(CHUNK)
    row = _iota((CHUNK, 1), 0)

    def chunk(c, carry):
        r0 = pl.multiple_of(c * CHUNK, CHUNK)
        rows = pl.ds(r0, CHUNK)
        valid = ((jb * rb + r0 + row) >= N_PAD).astype(F32)
        q = _silu(q_ref[rows, :].astype(F32))
        fr = f_ref[rows, :].astype(F32)
        sig = _sigmoid(fr)
        k = (1.0 - lb) * (1.0 - sig) * valid
        v = i_ref[rows, :].astype(F32)
        g2 = _mask_dot_rhs(tri, jnp.log(lb + (1.0 - lb) * sig)) * LOG2E
        qs[...] = q
        ks[...] = k
        gs[...] = g2

        diag = [[None] * n_sub for _ in heads]
        for sub in range(n_sub):
            base = sub * HG_SUB
            g_i, q_i, k_i = gs[base:base + HG_SUB, :], qs[base:base + HG_SUB, :], ks[base:base + HG_SUB, :]
            acc = [jnp.zeros((HG_SUB, H_HEAD_DIM), F32) for _ in heads]
            for d in range(HG_SUB):
                if d == 0:
                    e = q_i * k_i
                else:
                    dd = jnp.minimum(g_i - pltpu.roll(g_i, d, axis=0), 0.0)
                    e = jnp.exp2(dd) * (q_i * pltpu.roll(k_i, d, axis=0))
                at_key = col8 == key_row[d] + base
                for hd, hs in enumerate(heads):
                    s = jnp.sum(e[:, hs], axis=-1, keepdims=True)
                    acc[hd] = jnp.where(at_key, s, acc[hd])
            for hd in range(H_HEADS):
                diag[hd][sub] = acc[hd]

        g_last = g2[CHUNK - 1:CHUNK, :]
        qg = q * jnp.exp2(g2)
        kdec = k * jnp.exp2(g_last - g2)
        e_last = jnp.exp2(g_last)
        attn = []
        for hd, hs in enumerate(heads):
            g_h, q_h, k_h = g2[:, hs], q[:, hs], k[:, hs]
            qcat, kcat = [], []
            for sub in range(1, n_sub):
                base = sub * HG_SUB
                g_r = g_h[base:base + 1, :]
                q_sub = q_h[base:base + HG_SUB] * jnp.exp2(jnp.minimum(g_h[base:base + HG_SUB] - g_r, 0.0))
                pieces = [jnp.zeros((base, H_HEAD_DIM), F32), q_sub]
                if base + HG_SUB < CHUNK:
                    pieces.append(jnp.zeros((CHUNK - base - HG_SUB, H_HEAD_DIM), F32))
                qcat.append(jnp.concatenate(pieces, axis=0))
                k_sub = k_h[0:base] * jnp.exp2(jnp.minimum(g_r - g_h[0:base], 0.0))
                kcat.append(jnp.concatenate([k_sub, jnp.zeros((CHUNK - base, H_HEAD_DIM), F32)], axis=0))
            a_off = _dot(jnp.concatenate(qcat, axis=1), jnp.concatenate(kcat, axis=1), NT)
            a_diag = jnp.concatenate(diag[hd], axis=0)[:, 0:CHUNK]
            attn.append(jnp.where(off_mask, a_off, jnp.where(causal, a_diag, 0.0)))
        sts = [st_ref[hd] for hd in range(H_HEADS)]
        inter = [_dot(qg[:, hs], sts[hd], NT) for hd, hs in enumerate(heads)]
        upd = [_dot(v[:, hs], kdec[:, hs], TN) for hs in heads]
        intra = [_dot(attn[hd], v[:, hs]) for hd, hs in enumerate(heads)]
        outs = []
        for hd, hs in enumerate(heads):
            st_ref[hd] = sts[hd] * e_last[:, hs] + upd[hd]
            o_h = (intra[hd] + inter[hd]) * _sigmoid(g_ref[rows, hs].astype(F32))
            outs.append(o_h * lax.rsqrt(jnp.mean(o_h * o_h, axis=-1, keepdims=True) + NORM_EPS))
        o_ref[rows, :] = (jnp.concatenate(outs, axis=-1) * nw_ref[...]).astype(o_ref.dtype)
        return carry

    lax.fori_loop(0, rb // CHUNK, chunk, 0)


def _hgrn(p, lb_logits, norm_w, layer, batch, lp):
    rb = SEQ_BLOCK
    nb = lp // rb
    t = p.shape[0]
    cb = C_H // H_WIDTH

    def col(k):
        return pl.BlockSpec((rb, H_WIDTH), lambda b, j, k=k: (b * nb + j, cb + k))

    return pl.pallas_call(
        functools.partial(_hgrn_kernel, layer=layer, rb=rb),
        grid=(batch, nb),
        in_specs=[col(0), col(1), col(2), col(3),
                  pl.BlockSpec((DEPTH, H_WIDTH), lambda b, j: (0, 0)),
                  pl.BlockSpec((1, H_WIDTH), lambda b, j: (0, 0))],
        out_specs=pl.BlockSpec((rb, H_WIDTH), lambda b, j: (b * nb + j, 0)),
        out_shape=jax.ShapeDtypeStruct((t, H_WIDTH), BF16),
        scratch_shapes=[pltpu.VMEM((H_HEADS, H_HEAD_DIM, H_HEAD_DIM), F32)]
        + [pltpu.VMEM((CHUNK, H_WIDTH), F32) for _ in range(3)],
        compiler_params=_params(("parallel", "arbitrary")),
        name="hgrn2",
    )(p, p, p, p, lb_logits, norm_w)


HIST = 8


def _mamba_kernel(z_ref, x_ref, bc_ref, dt_ref, cwx_ref, cwb_ref, cbx_ref, cbb_ref,
                  dtb_ref, alog_ref, dsk_ref, nw_ref, o_ref,
                  st_ref, xe, be, xa, ba, xc_s, ac_s, lm_s, *, rb):
    jb = pl.program_id(1)

    @pl.when(jb == 0)
    def _():
        st_ref[...] = jnp.zeros(st_ref.shape, F32)
        xe[0:HIST, :] = jnp.zeros((HIST, M_WIDTH), F32)
        be[0:HIST, :] = jnp.zeros((HIST, 2 * M_BC), F32)

    def conv(ext, src_ref, w_ref, b_ref, dst):
        ext[HIST:HIST + rb, :] = src_ref[...].astype(F32)
        acc = b_ref[...]
        for tap in range(M_CONV):
            o = HIST - (M_CONV - 1) + tap
            acc = acc + w_ref[tap:tap + 1, :] * ext[o:o + rb, :]
        dst[...] = _silu(acc)
        ext[0:HIST, :] = ext[rb:rb + HIST, :]

    conv(xe, x_ref, cwx_ref, cbx_ref, xa)
    conv(be, bc_ref, cwb_ref, cbb_ref, ba)

    expand = (_iota((LANES, M_WIDTH), 1) // M_HEAD_DIM == _iota((LANES, M_WIDTH), 0)).astype(F32)
    pos_j = _iota((rb, M_WIDTH), 1) % CHUNK
    row_i = _iota((rb, M_WIDTH), 0) % CHUNK
    valid = ((jb * rb + _iota((rb, 1), 0)) >= N_PAD).astype(F32)
    dt = _softplus(dt_ref[...].astype(F32) + dtb_ref[...]) * valid
    da_e = _mask_dot_lhs(dt * -jnp.exp(alog_ref[...]), expand)
    acum = _mask_dot_rhs(_chunk_tri(rb), da_e)
    acum_j = _mask_dot_rhs(_chunk_tri(rb, lower=False), jnp.where(row_i <= pos_j, da_e, 0.0))
    lm_s[...] = jnp.where(row_i >= pos_j, jnp.exp(jnp.minimum(acum - acum_j, 0.0)), 0.0)
    ac_s[...] = acum
    xc_s[...] = xa[...] * _mask_dot_lhs(dt, expand)
    hpg = M_HEADS // M_GROUPS

    def chunk(c, carry):
        r0 = pl.multiple_of(c * CHUNK, CHUNK)
        rows = pl.ds(r0, CHUNK)
        xs = xa[rows, :]
        bcm = ba[rows, :]
        acum = ac_s[rows, :]
        lmat = lm_s[rows, :]
        a_last = acum[CHUNK - 1:CHUNK, :]
        e_cum = jnp.exp(acum)
        e_end = jnp.exp(a_last - acum)
        e_last = jnp.exp(a_last)
        xc = xc_s[rows, :]
        scores = []
        for g in range(M_GROUPS):
            b_g = bcm[:, g * M_STATE:(g + 1) * M_STATE]
            c_g = bcm[:, M_BC + g * M_STATE:M_BC + (g + 1) * M_STATE]
            scores.append(_dot(c_g, jnp.concatenate([b_g] * hpg, axis=0), NT))
        attn = jnp.concatenate(scores, axis=1) * lmat
        ys = []
        for pr in range(M_HEADS // 2):
            ps = slice(pr * LANES, (pr + 1) * LANES)
            g = (2 * pr) // hpg
            b_g = bcm[:, g * M_STATE:(g + 1) * M_STATE]
            c_g = bcm[:, M_BC + g * M_STATE:M_BC + (g + 1) * M_STATE]
            xc_p = xc[:, ps]
            st = st_ref[pr]
            y = _dot(attn[:, ps], _block_diag(xc_p)) + _dot(c_g, st) * e_cum[:, ps]
            st_ref[pr] = st * e_last[:, ps] + _dot(b_g, xc_p * e_end[:, ps], TN)
            ys.append(y)
        y = (jnp.concatenate(ys, axis=1) + dsk_ref[...] * xs) * _silu(z_ref[rows, :].astype(F32))
        gw = M_WIDTH // M_GROUPS
        outs = []
        for g in range(M_GROUPS):
            yg = y[:, g * gw:(g + 1) * gw]
            outs.append(yg * lax.rsqrt(jnp.mean(yg * yg, axis=-1, keepdims=True) + NORM_EPS))
        o_ref[rows, :] = (jnp.concatenate(outs, axis=1) * nw_ref[...]).astype(o_ref.dtype)
        return carry

    lax.fori_loop(0, rb // CHUNK, chunk, 0)


def _mamba(p, conv_w, conv_b, dt_bias, a_log, d_skip, norm_w, batch, lp):
    rb = SEQ_BLOCK
    nb = lp // rb
    t = p.shape[0]

    def col(off, width):
        return pl.BlockSpec((rb, width), lambda b, j: (b * nb + j, off // width))

    def full(shape):
        return pl.BlockSpec(shape, lambda b, j: (0,) * len(shape))

    pad = LANES - M_HEADS
    dtb = jnp.pad(dt_bias.reshape(1, M_HEADS), ((0, 0), (0, pad)))
    alog = jnp.pad(a_log.reshape(1, M_HEADS), ((0, 0), (0, pad)))
    dsk = jnp.repeat(d_skip, M_HEAD_DIM).reshape(1, M_WIDTH)
    return pl.pallas_call(
        functools.partial(_mamba_kernel, rb=rb),
        grid=(batch, nb),
        in_specs=[col(C_Z, M_WIDTH), col(C_X, M_WIDTH), col(C_BC, 2 * M_BC), col(C_DT, LANES),
                  full((M_CONV, M_WIDTH)), full((M_CONV, 2 * M_BC)),
                  full((1, M_WIDTH)), full((1, 2 * M_BC)),
                  full((1, LANES)), full((1, LANES)), full((1, M_WIDTH)), full((1, M_WIDTH))],
        out_specs=pl.BlockSpec((rb, M_WIDTH), lambda b, j: (b * nb + j, 0)),
        out_shape=jax.ShapeDtypeStruct((t, M_WIDTH), BF16),
        scratch_shapes=[pltpu.VMEM((M_HEADS // 2, M_STATE, LANES), F32),
                        pltpu.VMEM((rb + HIST, M_WIDTH), F32),
                        pltpu.VMEM((rb + HIST, 2 * M_BC), F32),
                        pltpu.VMEM((rb, M_WIDTH), F32),
                        pltpu.VMEM((rb, 2 * M_BC), F32)]
        + [pltpu.VMEM((rb, M_WIDTH), F32) for _ in range(3)],
        compiler_params=_params(("parallel", "arbitrary")),
        name="mamba2",
    )(p, p, p, p, conv_w[:, :M_WIDTH], conv_w[:, M_WIDTH:], conv_b[None, :M_WIDTH],
      conv_b[None, M_WIDTH:], dtb, alog, dsk, norm_w[None, :])


R_SHIFT_W = 3 * R_WIDTH
R_PRE = 2


def _rwkv_kernel(*refs, first, rb):
    if first:
        (rkv_ref, g1_ref, wa_ref, mu_rkv, mu_g1, mu_wa,
         w0_ref, w2_ref, a0_ref, a2_ref, g2_ref, kk_ref, ka_ref, rk_ref, gnw_ref, gnb_ref,
         o_ref, vf_out,
         st_ref, e_rkv, e_g1, e_wa, dec_s,
         lw_s, r_s, k_s, v_s, a_s, b_s, g_s, rt_s, wt_s, uv_s, arb_s, ov_s, pm_s, cm_s) = refs
    else:
        (rkv_ref, g1_ref, wa_ref, vr_ref, vf_ref, mu_rkv, mu_g1, mu_wa, mu_vr,
         w0_ref, w2_ref, a0_ref, a2_ref, g2_ref, kk_ref, ka_ref, rk_ref, gnw_ref, gnb_ref,
         v0_ref, v2_ref,
         o_ref,
         st_ref, e_rkv, e_g1, e_wa, e_vr, dec_s,
         lw_s, r_s, k_s, v_s, a_s, b_s, g_s, rt_s, wt_s, uv_s, arb_s, ov_s, pm_s, cm_s) = refs
    jb = pl.program_id(1)
    exts = [e_rkv, e_g1, e_wa] + ([] if first else [e_vr])

    @pl.when(jb == 0)
    def _():
        st_ref[...] = jnp.zeros(st_ref.shape, F32)
        for ext in exts:
            ext[0:HIST, :] = jnp.zeros((HIST, ext.shape[1]), F32)

    def shift(ext, src_ref, mu_ref):
        ext[HIST:HIST + rb, :] = src_ref[...].astype(F32)
        cur = ext[HIST:HIST + rb, :]
        prev = ext[HIST - 1:HIST - 1 + rb, :]
        out = cur + (prev - cur) * mu_ref[...]
        ext[0:HIST, :] = ext[rb:rb + HIST, :]
        return out

    rkv = shift(e_rkv, rkv_ref, mu_rkv)
    gl = shift(e_g1, g1_ref, mu_g1)
    wa = shift(e_wa, wa_ref, mu_wa)
    r = rkv[:, 0:R_WIDTH]
    k = rkv[:, R_WIDTH:2 * R_WIDTH]
    v = rkv[:, 2 * R_WIDTH:3 * R_WIDTH]
    wl = wa[:, 0:LANES]
    al = wa[:, LANES:2 * LANES]

    valid = ((jb * rb + _iota((rb, 1), 0)) >= N_PAD).astype(F32)
    w_log = -_softplus(-(w0_ref[...] + _dot(jnp.tanh(wl), w2_ref[...]))) - 0.5
    lw_s[...] = -jnp.exp(w_log)
    a = _sigmoid(a0_ref[...] + _dot(al, a2_ref[...]))
    if first:
        vf_out[...] = v
    else:
        vl = shift(e_vr, vr_ref, mu_vr)
        v = v + (vf_ref[...] - v) * _sigmoid(v0_ref[...] + _dot(vl, v2_ref[...]))
    g_s[...] = _dot(_sigmoid(gl), g2_ref[...])

    seg_ones = (_iota((LANES, LANES), 0) // R_HEAD_DIM == _iota((LANES, LANES), 1) // R_HEAD_DIM).astype(F32)

    def head_sum(x):
        return jnp.concatenate(
            [_mask_dot_lhs(x[:, s * LANES:(s + 1) * LANES], seg_ones, terms=2) for s in range(N_PAIRS)], axis=1)

    kk = k * kk_ref[...]
    kk = kk * lax.rsqrt(jnp.maximum(head_sum(kk * kk), L2_EPS))
    kh = k * (1.0 + (a - 1.0) * ka_ref[...]) * valid
    r_s[...] = r
    k_s[...] = kh
    v_s[...] = v
    a_s[...] = -kk
    b_s[...] = kk * a

    tri = _tri(CHUNK)
    t_i = _iota((CHUNK, LANES), 0)
    s_j = _iota((CHUNK, LANES), 1) % CHUNK
    strict = t_i > s_j
    incl = t_i >= s_j
    eye = (t_i == s_j).astype(F32)
    same_head = _iota((LANES, LANES), 0) // R_HEAD_DIM == _iota((LANES, LANES), 1) // R_HEAD_DIM
    n_double = 5
    assert 2 ** (n_double + 1) == CHUNK

    pairs = [slice(pr * LANES, (pr + 1) * LANES) for pr in range(N_PAIRS)]

    def precompute(cc):
        units = []
        for sub in range(R_PRE):
            c = cc * R_PRE + sub
            rows = slice(c * CHUNK, (c + 1) * CHUNK)
            lw = lw_s[rows, :]
            cum = _mask_dot_rhs(tri, lw)
            c_last = cum[CHUNK - 1:CHUNK, :]
            e_inv = jnp.exp(-cum)
            e_end = jnp.exp(c_last - cum)
            b_c, k_c, v_c = b_s[rows, :], k_s[rows, :], v_s[rows, :]
            a_t = a_s[rows, :] * jnp.exp(cum - lw)
            r_t = r_s[rows, :] * jnp.exp(cum)
            b_t = b_c * e_inv
            k_t = k_c * e_inv
            b_h = b_c * e_end
            k_h = k_c * e_end
            rt_s[rows, :] = r_t
            dec_s[c:c + 1, :] = jnp.exp(c_last)
            units += [(rows, ps, a_t[:, ps], r_t[:, ps], b_t[:, ps], k_t[:, ps], v_c[:, ps],
                       b_h[:, ps], k_h[:, ps], c * N_PAIRS + pr) for pr, ps in enumerate(pairs)]
        m = [_dot(jnp.concatenate([a_t, r_t], axis=0),
                  jnp.concatenate([_block_diag(b_t), _block_diag(k_t)], axis=0), NT)
             for (_, _, a_t, r_t, b_t, k_t, _, _, _, _) in units]
        a_ab = [jnp.where(strict, x[0:CHUNK, 0:LANES], 0.0) for x in m]
        a_ak = [jnp.where(strict, x[0:CHUNK, LANES:2 * LANES], 0.0) for x in m]
        a_rk = [jnp.where(incl, x[CHUNK:2 * CHUNK, LANES:2 * LANES], 0.0) for x in m]
        for x, un in zip(m, units):
            arb_s[un[0], un[1]] = jnp.where(incl, x[CHUNK:2 * CHUNK, 0:LANES], 0.0)
        inv = [eye + x for x in a_ab]
        pw = [_dot(x, _block_diag(x)) for x in a_ab]
        for it in range(n_double - 1):
            both = [_dot(jnp.concatenate([p_, i_], axis=0), _block_diag(p_)) for p_, i_ in zip(pw, inv)]
            pw = [x[0:CHUNK] for x in both]
            inv = [i_ + x[CHUNK:2 * CHUNK] for i_, x in zip(inv, both)]
        inv = [i_ + _dot(i_, _block_diag(p_)) for i_, p_ in zip(inv, pw)]
        t1ov = [_dot(jnp.concatenate([x, y], axis=0), _block_diag(un[6])) for x, y, un in zip(a_ak, a_rk, units)]
        wtuv = [_dot(i_, jnp.concatenate([_block_diag(un[2]), _block_diag(x[0:CHUNK])], axis=1))
                for i_, un, x in zip(inv, units, t1ov)]
        wt = [x[:, 0:LANES] for x in wtuv]
        uv = [x[:, LANES:2 * LANES] for x in wtuv]
        pm = [_dot(w_, un[7], TN) for w_, un in zip(wt, units)]
        cm = [_dot(jnp.concatenate([x, un[6]], axis=0), jnp.concatenate([un[7], un[8]], axis=0), TN)
              for x, un in zip(uv, units)]
        for i, un in enumerate(units):
            wt_s[un[0], un[1]] = wt[i]
            uv_s[un[0], un[1]] = uv[i]
            ov_s[un[0], un[1]] = t1ov[i][CHUNK:2 * CHUNK]
            pm_s[un[9]] = jnp.where(same_head, pm[i], 0.0)
            cm_s[un[9]] = jnp.where(same_head, cm[i], 0.0)

    n_chunks = rb // CHUNK
    assert n_chunks % R_PRE == 0
    for cc in range(n_chunks // R_PRE):
        precompute(cc)

    inv_n = 1.0 / R_HEAD_DIM

    def finish(rows, o):
        mu = head_sum(o) * inv_n
        d = o - mu
        var = head_sum(d * d) * inv_n
        o = d * lax.rsqrt(var + R_GN_EPS) * gnw_ref[...] + gnb_ref[...]
        o = o + head_sum(r_s[rows, :] * k_s[rows, :] * rk_ref[...]) * v_s[rows, :]
        o_ref[rows, :] = (o * g_s[rows, :]).astype(o_ref.dtype)

    sts = [st_ref[pr] for pr in range(N_PAIRS)]
    pending = None
    for c in range(n_chunks):
        rows = slice(c * CHUNK, (c + 1) * CHUNK)
        dec = dec_s[c:c + 1, :]
        new_sts = [st * dec[:, ps] + _dot(st, pm_s[c * N_PAIRS + pr]) + cm_s[c * N_PAIRS + pr]
                   for pr, (ps, st) in enumerate(zip(pairs, sts))]
        uo = [_dot(jnp.concatenate([wt_s[rows, ps], rt_s[rows, ps]], axis=0), st, NT)
              for ps, st in zip(pairs, sts)]
        u = [x[0:CHUNK] + uv_s[rows, ps] for x, ps in zip(uo, pairs)]
        o1 = [x[CHUNK:2 * CHUNK] for x in uo]
        o2 = [_dot(arb_s[rows, ps], _block_diag(u_)) for ps, u_ in zip(pairs, u)]
        sts = new_sts
        if pending is not None:
            finish(*pending)
        pending = (rows, jnp.concatenate(
            [o1[pr] + o2[pr] + ov_s[rows, ps] for pr, ps in enumerate(pairs)], axis=1))
    finish(*pending)
    for pr in range(N_PAIRS):
        st_ref[pr] = sts[pr]


def _rwkv(p, v_first, prm, first, batch, lp):
    rb = SEQ_BLOCK
    nb = lp // rb
    t = p.shape[0]

    def col(off, width):
        return pl.BlockSpec((rb, width), lambda b, j: (b * nb + j, off // width))

    def full(a):
        return pl.BlockSpec(a.shape, lambda b, j: (0,) * a.ndim)

    row_spec = pl.BlockSpec((rb, R_WIDTH), lambda b, j: (b * nb + j, 0))
    acts = [p, p, p]
    act_specs = [col(C_RKV, R_SHIFT_W), col(C_G1, R_GATE_RANK), col(C_WA, 2 * LANES)]
    mus = [prm["mu_rkv"], prm["mu_g1"], prm["mu_wa"]]
    tail = []
    if not first:
        acts += [p, v_first]
        act_specs += [col(C_VRES, LANES), row_spec]
        mus.append(prm["mu_vr"])
        tail = [prm["v0"], prm["v2"]]
    consts = mus + [prm[n] for n in ("w0", "w2", "a0", "a2", "g2", "k_k", "k_a", "r_k", "gn_w", "gn_b")] + tail
    out_shape = [jax.ShapeDtypeStruct((t, R_WIDTH), BF16)]
    out_specs = [row_spec]
    if first:
        out_shape.append(jax.ShapeDtypeStruct((t, R_WIDTH), F32))
        out_specs.append(row_spec)
    ext_w = [R_SHIFT_W, R_GATE_RANK, 2 * LANES] + ([] if first else [LANES])
    scratch = ([pltpu.VMEM((N_PAIRS, LANES, LANES), F32)]
               + [pltpu.VMEM((rb + HIST, w), F32) for w in ext_w]
               + [pltpu.VMEM((max(rb // CHUNK, 8), R_WIDTH), F32)]
               + [pltpu.VMEM((rb, R_WIDTH), F32) for _ in range(12)]
               + [pltpu.VMEM((rb // CHUNK * N_PAIRS, LANES, LANES), F32) for _ in range(2)])
    return pl.pallas_call(
        functools.partial(_rwkv_kernel, first=first, rb=rb),
        grid=(batch, nb),
        in_specs=act_specs + [full(a) for a in consts],
        out_specs=out_specs,
        out_shape=out_shape,
        scratch_shapes=scratch,
        compiler_params=_params(("parallel", "arbitrary")),
        name="rwkv7",
    )(*acts, *consts)


def _merge_kernel(oh_ref, om_ref, or_ref, g0_ref, g1_ref, g2_ref, wh_ref, wm_ref, wr_ref, o_ref):
    def gated(g_ref, x_ref, w_ref):
        return _sigmoid(g_ref[...].astype(F32)) * jnp.dot(x_ref[...], w_ref[...], preferred_element_type=F32)

    acc = gated(g0_ref, oh_ref, wh_ref) + gated(g1_ref, om_ref, wm_ref) + gated(g2_ref, or_ref, wr_ref)
    o_ref[...] = acc.astype(o_ref.dtype)


def _merge(o_h, o_m, o_r, p, w_h, w_m, w_r, tm, tn):
    t = p.shape[0]
    gb = C_GATE // tn
    nt = D_MODEL // tn

    def act():
        return pl.BlockSpec((tm, H_WIDTH), lambda i, j: (i, 0))

    def gate(k):
        return pl.BlockSpec((tm, tn), lambda i, j, k=k: (i, gb + k * nt + j))

    def wt():
        return pl.BlockSpec((H_WIDTH, tn), lambda i, j: (0, j))

    return pl.pallas_call(
        _merge_kernel,
        grid=(t // tm, nt),
        in_specs=[act(), act(), act(), gate(0), gate(1), gate(2), wt(), wt(), wt()],
        out_specs=pl.BlockSpec((tm, tn), lambda i, j: (i, j)),
        out_shape=jax.ShapeDtypeStruct((t, D_MODEL), BF16),
        compiler_params=_params(("parallel", "arbitrary")),
        name="merge",
    )(o_h, o_m, o_r, p, p, p, w_h, w_m, w_r)


def _row_valid(tm, lp):
    pos = (pl.program_id(0) * tm + _iota((tm, 1), 0)) % lp
    return (pos >= N_PAD).astype(F32)


def _tail_kernel(h_ref, m_ref, wo_ref, lnw_ref, w1_ref, w2_ref, lnf_ref, o_ref, u_ref, *, tm, lp, final):
    kk = pl.program_id(1)

    @pl.when(kk == 0)
    def _():
        valid = _row_valid(tm, lp)
        mixed = jnp.where(valid > 0.0, m_ref[...], jnp.zeros((), m_ref.dtype))
        x = h_ref[...] + jnp.dot(mixed, wo_ref[...], preferred_element_type=F32)
        ms = jnp.mean(x * x, axis=-1, keepdims=True)
        u_ref[...] = (x * lax.rsqrt(ms + NORM_EPS) * lnw_ref[...] * valid).astype(BF16)
        o_ref[...] = x

    a = jnp.maximum(jnp.dot(u_ref[...], w1_ref[...], preferred_element_type=F32), 0.0)
    o_ref[...] += jnp.dot((a * a).astype(BF16), w2_ref[...], preferred_element_type=F32)

    if final:
        @pl.when(kk == pl.num_programs(1) - 1)
        def _():
            y = o_ref[...]
            ms = jnp.mean(y * y, axis=-1, keepdims=True)
            o_ref[...] = y * lax.rsqrt(ms + NORM_EPS) * lnf_ref[...]


def _tail(h, mixed, w_out, lnw, w1, w2, lnf, tm, th, lp, final):
    t = h.shape[0]
    return pl.pallas_call(
        functools.partial(_tail_kernel, tm=tm, lp=lp, final=final),
        grid=(t // tm, MLP_HIDDEN // th),
        in_specs=[pl.BlockSpec((tm, D_MODEL), lambda i, k: (i, 0)),
                  pl.BlockSpec((tm, D_MODEL), lambda i, k: (i, 0)),
                  pl.BlockSpec((D_MODEL, D_MODEL), lambda i, k: (0, 0), pipeline_mode=pl.Buffered(1)),
                  pl.BlockSpec((1, D_MODEL), lambda i, k: (0, 0)),
                  pl.BlockSpec((D_MODEL, th), lambda i, k: (0, k)),
                  pl.BlockSpec((th, D_MODEL), lambda i, k: (k, 0)),
                  pl.BlockSpec((1, D_MODEL), lambda i, k: (0, 0))],
        out_specs=pl.BlockSpec((tm, D_MODEL), lambda i, k: (i, 0)),
        out_shape=jax.ShapeDtypeStruct((t, D_MODEL), F32),
        scratch_shapes=[pltpu.VMEM((tm, D_MODEL), BF16)],
        compiler_params=_params(("parallel", "arbitrary")),
        name="out_mlp",
    )(h, mixed, w_out, lnw, w1, w2, lnf)


def _pad_cols(a, width):
    return jnp.pad(a, ((0, 0), (0, width - a.shape[1])))


def _pad_rows(a, height):
    return jnp.pad(a, ((0, height - a.shape[0]), (0, 0)))


def _split_in_cols(a, a_vres):
    o = 0
    hcols = a[:, o:o + H_COLS]; o += H_COLS
    z = a[:, o:o + M_WIDTH]; o += M_WIDTH
    xm = a[:, o:o + M_WIDTH]; o += M_WIDTH
    bc = a[:, o:o + 2 * M_BC]; o += 2 * M_BC
    dt = a[:, o:o + M_HEADS]; o += M_HEADS
    rkv = a[:, o:o + 3 * R_WIDTH]; o += 3 * R_WIDTH
    w1 = a[:, o:o + R_DECAY_RANK]; o += R_DECAY_RANK
    a1 = a[:, o:o + R_AAA_RANK]; o += R_AAA_RANK
    g1 = a[:, o:o + R_GATE_RANK]; o += R_GATE_RANK
    gates = a[:, o:o + GATE_COLS]; o += GATE_COLS
    assert o == IN_COLS
    out = jnp.concatenate([hcols, gates, z, xm, rkv, bc, g1, _pad_cols(w1, LANES), _pad_cols(a1, LANES),
                           _pad_cols(dt, LANES), _pad_cols(a_vres, LANES)], axis=1)
    assert out.shape[1] == P_COLS
    return out


def _tile(t, candidates):
    for c in candidates:
        if t % c == 0:
            return c
    raise ValueError(f"no tile for {t}")


def kernel(x, meta, ln1_w, ln2_w, lnf_w, w_in, w_in_vres, hg_lb_logits, hg_norm_w, m_conv_w, m_conv_b, m_dt_bias, m_a_log, m_d, m_norm_w, r_mu, r_mu_vres, r_w0, r_w2, r_a0, r_a2, r_v0, r_v2, r_g2, r_k_k, r_k_a, r_r_k, r_gn_w, r_gn_b, w_up_h, w_up_m, w_up_r, w_out, w_mlp_in, w_mlp_out):
    batch, seq, d = x.shape
    assert d == D_MODEL and M_HEAD_DIM == CHUNK and R_HEAD_DIM == CHUNK
    l_real = seq + CHUNK
    lp = -(-l_real // ROW_ALIGN) * ROW_ALIGN
    t = batch * lp
    tm_proj = _tile(t, (1280, 640))
    tm = _tile(t, (640,))
    tm_tail = _tile(t, (512, 256))

    h = jnp.concatenate([jnp.zeros((batch, N_PAD, d), x.dtype),
                         jnp.broadcast_to(meta.astype(x.dtype), (batch, N_META, d)),
                         x, jnp.zeros((batch, lp - l_real, d), x.dtype)], axis=1).reshape(t, d)

    v_first = None
    for l in range(DEPTH):
        vres_w = w_in_vres[l - 1] if l > 0 else jnp.zeros((d, R_MV_RANK), w_in.dtype)
        w_comb = _split_in_cols(w_in[l], vres_w).astype(BF16)
        p = _proj(h, ln1_w[l][None, :], w_comb, tm_proj, 1280)

        o_h = _hgrn(p, hg_lb_logits, hg_norm_w[l].reshape(1, H_WIDTH), l, batch, lp)
        o_m = _mamba(p, m_conv_w[l], m_conv_b[l], m_dt_bias[l], m_a_log[l], m_d[l], m_norm_w[l], batch, lp)

        mu = r_mu[l][None, :]
        o = 3 * R_WIDTH
        prm = {
            "mu_rkv": mu[:, :o],
            "mu_wa": jnp.concatenate([_pad_cols(mu[:, o:o + R_DECAY_RANK], LANES),
                                      _pad_cols(mu[:, o + R_DECAY_RANK:o + R_DECAY_RANK + R_AAA_RANK], LANES)], axis=1),
            "mu_g1": mu[:, o + R_DECAY_RANK + R_AAA_RANK:],
            "w0": r_w0[l][None, :], "w2": _pad_rows(r_w2[l], LANES).astype(BF16),
            "a0": r_a0[l][None, :], "a2": _pad_rows(r_a2[l], LANES).astype(BF16),
            "g2": r_g2[l].astype(BF16),
            "k_k": r_k_k[l][None, :], "k_a": r_k_a[l][None, :], "r_k": r_r_k[l].reshape(1, R_WIDTH),
            "gn_w": r_gn_w[l][None, :], "gn_b": r_gn_b[l][None, :],
        }
        if l == 0:
            o_r, v_first = _rwkv(p, None, prm, True, batch, lp)
        else:
            prm["mu_vr"] = _pad_cols(r_mu_vres[l - 1][None, :], LANES)
            prm["v0"] = r_v0[l - 1][None, :]
            prm["v2"] = _pad_rows(r_v2[l - 1], LANES).astype(BF16)
            (o_r,) = _rwkv(p, v_first, prm, False, batch, lp)

        mixed = _merge(o_h, o_m, o_r, p, w_up_h[l].astype(BF16), w_up_m[l].astype(BF16),
                       w_up_r[l].astype(BF16), tm, 1024)
        h = _tail(h, mixed, w_out[l].astype(BF16), ln2_w[l][None, :], w_mlp_in[l].astype(BF16),
                  w_mlp_out[l].astype(BF16), lnf_w[None, :], tm_tail, 1024, lp, l == DEPTH - 1)

    return h.reshape(batch, lp, d)[:, CHUNK:l_real]
```

```python
import functools

import jax
import jax.numpy as jnp
from jax import lax
from jax.experimental import pallas as pl
from jax.experimental.pallas import tpu as pltpu

F32 = jnp.float32
BF16 = jnp.bfloat16

LANES = 128

D_MODEL = 2048
DEPTH = 2
N_META = 16
CHUNK = 64
N_PAD = CHUNK - N_META
MLP_HIDDEN = 4 * D_MODEL
NORM_EPS = 1e-6
N_BRANCH = 3
L2_EPS = 1e-24

H_WIDTH = D_MODEL // 2
H_HEAD_DIM = 128
H_HEADS = H_WIDTH // H_HEAD_DIM

M_WIDTH = D_MODEL // 2
M_HEAD_DIM = 64
M_HEADS = M_WIDTH // M_HEAD_DIM
M_GROUPS = 2
M_STATE = 128
M_CONV = 4
M_BC = M_GROUPS * M_STATE

R_WIDTH = D_MODEL // 2
R_HEAD_DIM = 64
R_HEADS = R_WIDTH // R_HEAD_DIM
R_DECAY_RANK = max(32, int(round(1.8 * D_MODEL ** 0.5 / 32)) * 32)
R_AAA_RANK = max(32, int(round(1.8 * D_MODEL ** 0.5 / 32)) * 32)
R_MV_RANK = max(32, int(round(1.3 * D_MODEL ** 0.5 / 32)) * 32)
R_GATE_RANK = max(32, int(round(0.6 * D_MODEL ** 0.8 / 32)) * 32)
R_GN_EPS = 64e-5
N_PAIRS = R_HEADS // 2

H_COLS = 4 * H_WIDTH
M_CONV_CH = M_WIDTH + 2 * M_BC
M_COLS = M_WIDTH + M_CONV_CH + M_HEADS
R_COLS = 3 * R_WIDTH + R_DECAY_RANK + R_AAA_RANK + R_GATE_RANK
GATE_COLS = N_BRANCH * D_MODEL
IN_COLS = H_COLS + M_COLS + R_COLS + GATE_COLS

C_H = 0
C_GATE = C_H + H_COLS
C_Z = C_GATE + GATE_COLS
C_X = C_Z + M_WIDTH
C_RKV = C_X + M_WIDTH
C_BC = C_RKV + 3 * R_WIDTH
C_G1 = C_BC + 2 * M_BC
C_WA = C_G1 + R_GATE_RANK
C_DT = C_WA + 2 * LANES
C_VRES = C_DT + LANES
P_COLS = C_VRES + LANES

ROW_ALIGN = 1280
SEQ_BLOCK = 256
HG_BLOCK = 640
VMEM_LIMIT = 56 * 1024 * 1024

NN = (((1,), (0,)), ((), ()))
NT = (((1,), (1,)), ((), ()))
TN = (((0,), (0,)), ((), ()))


def _dot(a, b, dims=NN):
    return lax.dot_general(a.astype(BF16), b.astype(BF16), dims, preferred_element_type=F32)


def _split(x, terms):
    out = []
    for _ in range(terms - 1):
        hi = x.astype(BF16)
        out.append(hi)
        x = x - hi.astype(F32)
    out.append(x.astype(BF16))
    return out


def _mask_dot_rhs(m, x, terms=3):
    mb = m.astype(BF16)
    return jnp.dot(jnp.concatenate([mb] * terms, axis=1), jnp.concatenate(_split(x, terms), axis=0),
                   preferred_element_type=F32)


def _mask_dot_lhs(x, m, terms=3):
    mb = m.astype(BF16)
    return jnp.dot(jnp.concatenate(_split(x, terms), axis=1), jnp.concatenate([mb] * terms, axis=0),
                   preferred_element_type=F32)


LOG2E = 1.4426950408889634


def _sigmoid(x):
    return 1.0 / (1.0 + jnp.exp2(x * -LOG2E))


def _silu(x):
    return x * _sigmoid(x)


def _softplus(x):
    return jnp.maximum(x, 0.0) + jnp.log(1.0 + jnp.exp(-jnp.abs(x)))


def _iota(shape, dim):
    return lax.broadcasted_iota(jnp.int32, shape, dim)


def _tri(n):
    return (_iota((n, n), 0) >= _iota((n, n), 1)).astype(F32)


def _chunk_tri(n, lower=True):
    r, c = _iota((n, n), 0), _iota((n, n), 1)
    same = r // CHUNK == c // CHUNK
    return (same & (r >= c) if lower else same).astype(F32)


def _head_masks():
    lane = _iota((CHUNK, LANES), 1)
    return lane < 64, lane >= 64


def _block_diag(x, masks):
    return jnp.concatenate([jnp.where(masks[0], x, 0.0), jnp.where(masks[1], x, 0.0)], axis=0)


def _params(sem):
    return pltpu.CompilerParams(dimension_semantics=sem, vmem_limit_bytes=VMEM_LIMIT)


def _proj_kernel(h_ref, lnw_ref, w_ref, o_ref, u_ref):
    @pl.when(pl.program_id(1) == 0)
    def _():
        x = h_ref[...]
        ms = jnp.mean(x * x, axis=-1, keepdims=True)
        u_ref[...] = (x * lax.rsqrt(ms + NORM_EPS) * lnw_ref[...]).astype(BF16)

    o_ref[...] = jnp.dot(u_ref[...], w_ref[...], preferred_element_type=F32).astype(o_ref.dtype)


def _proj(h, lnw, w, tm, tn):
    t = h.shape[0]
    return pl.pallas_call(
        _proj_kernel,
        grid=(t // tm, P_COLS // tn),
        in_specs=[
            pl.BlockSpec((tm, D_MODEL), lambda i, j: (i, 0)),
            pl.BlockSpec((1, D_MODEL), lambda i, j: (0, 0)),
            pl.BlockSpec((D_MODEL, tn), lambda i, j: (0, j)),
        ],
        out_specs=pl.BlockSpec((tm, tn), lambda i, j: (i, j)),
        out_shape=jax.ShapeDtypeStruct((t, P_COLS), BF16),
        scratch_shapes=[pltpu.VMEM((tm, D_MODEL), BF16)],
        compiler_params=_params(("parallel", "arbitrary")),
        name="in_proj",
    )(h, lnw, w)


HG_SUB = 8


def _hgrn_kernel(q_ref, f_ref, i_ref, g_ref, lbl_ref, nw_ref, o_ref,
                 st_ref, qs, ks, gs, hs_, *, layer, rb):
    jb = pl.program_id(1)

    @pl.when(jb == 0)
    def _():
        st_ref[...] = jnp.zeros(st_ref.shape, F32)

    lg = lbl_ref[...]
    mx = jnp.max(lg, axis=0, keepdims=True)
    ex = jnp.exp(lg - mx)
    sm = ex / jnp.sum(ex, axis=0, keepdims=True)
    lb = jnp.sum(sm[0:layer + 1], axis=0, keepdims=True) - sm[0:1]

    assert HG_SUB == 8
    col8 = _iota((HG_SUB, H_HEAD_DIM), 1)
    key_row = [(_iota((HG_SUB, H_HEAD_DIM), 0) + (HG_SUB - d)) % HG_SUB for d in range(HG_SUB)]
    heads = [slice(hd * H_HEAD_DIM, (hd + 1) * H_HEAD_DIM) for hd in range(H_HEADS)]
    arow = _iota((CHUNK, CHUNK), 0)
    acol = _iota((CHUNK, CHUNK), 1)
    off_mask = acol < (arow // HG_SUB) * HG_SUB
    causal = arow >= acol
    n_sub = CHUNK // HG_SUB

    tri = _tri(CHUNK)
    row = _iota((CHUNK, 1), 0)

    def chunk(c, carry):
        r0 = pl.multiple_of(c * CHUNK, CHUNK)
        rows = pl.ds(r0, CHUNK)
        valid = ((jb * rb + r0 + row) >= N_PAD).astype(F32)
        q = _silu(q_ref[rows, :].astype(F32))
        fr = f_ref[rows, :].astype(F32)
        sig = _sigmoid(fr)
        k = (1.0 - lb) * (1.0 - sig) * valid
        v = i_ref[rows, :].astype(F32)
        g2 = _mask_dot_rhs(tri, jnp.log(lb + (1.0 - lb) * sig)) * LOG2E
        hk = g2 - jnp.log(k) * LOG2E
        qs[...] = q
        ks[...] = k
        gs[...] = g2
        hs_[...] = hk

        diag = [[None] * n_sub for _ in heads]
        for sub in range(n_sub):
            base = sub * HG_SUB
            g_i, q_i, k_i = gs[base:base + HG_SUB, :], qs[base:base + HG_SUB, :], ks[base:base + HG_SUB, :]
            h_i = hs_[base:base + HG_SUB, :]
            acc = [jnp.zeros((HG_SUB, H_HEAD_DIM), F32) for _ in heads]
            for d in range(HG_SUB):
                if d == 0:
                    e = q_i * k_i
                else:
                    e = jnp.exp2(jnp.minimum(g_i - pltpu.roll(h_i, d, axis=0), 0.0)) * q_i
                at_key = col8 == key_row[d] + base
                for hd, hs in enumerate(heads):
                    s = jnp.sum(e[:, hs], axis=-1, keepdims=True)
                    acc[hd] = jnp.where(at_key, s, acc[hd])
            for hd in range(H_HEADS):
                diag[hd][sub] = acc[hd]

        g_last = g2[CHUNK - 1:CHUNK, :]
        qg = q * jnp.exp2(g2)
        kdec = jnp.exp2(g_last - hk)
        e_last = jnp.exp2(g_last)
        attn = []
        for hd, hs in enumerate(heads):
            g_h, q_h, hk_h = g2[:, hs], q[:, hs], hk[:, hs]
            qcat, kcat = [], []
            for sub in range(1, n_sub):
                base = sub * HG_SUB
                g_r = g_h[base:base + 1, :]
                q_sub = q_h[base:base + HG_SUB] * jnp.exp2(jnp.minimum(g_h[base:base + HG_SUB] - g_r, 0.0))
                pieces = [jnp.zeros((base, H_HEAD_DIM), F32), q_sub]
                if base + HG_SUB < CHUNK:
                    pieces.append(jnp.zeros((CHUNK - base - HG_SUB, H_HEAD_DIM), F32))
                qcat.append(jnp.concatenate(pieces, axis=0))
                k_sub = jnp.exp2(jnp.minimum(g_r - hk_h[0:base], 0.0))
                kcat.append(jnp.concatenate([k_sub, jnp.zeros((CHUNK - base, H_HEAD_DIM), F32)], axis=0))
            a_off = _dot(jnp.concatenate(qcat, axis=1), jnp.concatenate(kcat, axis=1), NT)
            a_diag = jnp.concatenate(diag[hd], axis=0)[:, 0:CHUNK]
            attn.append(jnp.where(off_mask, a_off, jnp.where(causal, a_diag, 0.0)))
        sts = [st_ref[hd] for hd in range(H_HEADS)]
        inter = [_dot(qg[:, hs], sts[hd], NT) for hd, hs in enumerate(heads)]
        upd = [_dot(v[:, hs], kdec[:, hs], TN) for hs in heads]
        intra = [_dot(attn[hd], v[:, hs]) for hd, hs in enumerate(heads)]
        outs = []
        for hd, hs in enumerate(heads):
            st_ref[hd] = sts[hd] * e_last[:, hs] + upd[hd]
            o_h = (intra[hd] + inter[hd]) * _sigmoid(g_ref[rows, hs].astype(F32))
            outs.append(o_h * lax.rsqrt(jnp.mean(o_h * o_h, axis=-1, keepdims=True) + NORM_EPS))
        o_ref[rows, :] = (jnp.concatenate(outs, axis=-1) * nw_ref[...]).astype(o_ref.dtype)
        return carry

    lax.fori_loop(0, rb // CHUNK, chunk, 0)


def _hgrn(p, lb_logits, norm_w, layer, batch, lp):
    rb = HG_BLOCK
    nb = lp // rb
    t = p.shape[0]
    cb = C_H // H_WIDTH

    def col(k):
        return pl.BlockSpec((rb, H_WIDTH), lambda b, j, k=k: (b * nb + j, cb + k))

    return pl.pallas_call(
        functools.partial(_hgrn_kernel, layer=layer, rb=rb),
        grid=(batch, nb),
        in_specs=[col(0), col(1), col(2), col(3),
                  pl.BlockSpec((DEPTH, H_WIDTH), lambda b, j: (0, 0)),
                  pl.BlockSpec((1, H_WIDTH), lambda b, j: (0, 0))],
        out_specs=pl.BlockSpec((rb, H_WIDTH), lambda b, j: (b * nb + j, 0)),
        out_shape=jax.ShapeDtypeStruct((t, H_WIDTH), BF16),
        scratch_shapes=[pltpu.VMEM((H_HEADS, H_HEAD_DIM, H_HEAD_DIM), F32)]
        + [pltpu.VMEM((CHUNK, H_WIDTH), F32) for _ in range(4)],
        compiler_params=_params(("parallel", "arbitrary")),
        name="hgrn2",
    )(p, p, p, p, lb_logits, norm_w)


HIST = 8


def _mamba_kernel(z_ref, x_ref, bc_ref, dt_ref, cwx_ref, cwb_ref, cbx_ref, cbb_ref,
                  dtb_ref, alog_ref, dsk_ref, nw_ref, expand_ref, tri_ref, same_ref, le_ref, ge_ref, o_ref,
                  st_ref, xe, be, xa, ba, xc_s, ac_s, lm_s, *, rb):
    jb = pl.program_id(1)

    @pl.when(jb == 0)
    def _():
        st_ref[...] = jnp.zeros(st_ref.shape, F32)
        xe[0:HIST, :] = jnp.zeros((HIST, M_WIDTH), F32)
        be[0:HIST, :] = jnp.zeros((HIST, 2 * M_BC), F32)

    def conv(ext, src_ref, w_ref, b_ref, dst):
        ext[HIST:HIST + rb, :] = src_ref[...].astype(F32)
        acc = b_ref[...]
        for tap in range(M_CONV):
            o = HIST - (M_CONV - 1) + tap
            acc = acc + w_ref[tap:tap + 1, :] * ext[o:o + rb, :]
        dst[...] = _silu(acc)
        ext[0:HIST, :] = ext[rb:rb + HIST, :]

    conv(xe, x_ref, cwx_ref, cbx_ref, xa)
    conv(be, bc_ref, cwb_ref, cbb_ref, ba)

    expand = expand_ref[...]
    valid = ((jb * rb + _iota((rb, 1), 0)) >= N_PAD).astype(F32)
    dt = _softplus(dt_ref[...].astype(F32) + dtb_ref[...]) * valid
    da_e = _mask_dot_lhs(dt * -jnp.exp(alog_ref[...]), expand)
    acum = _mask_dot_rhs(tri_ref[...], da_e)
    acum_j = _mask_dot_rhs(same_ref[...], da_e * le_ref[...])
    lm_s[...] = jnp.exp(jnp.minimum(acum - acum_j, 0.0)) * ge_ref[...]
    ac_s[...] = acum
    xc_s[...] = xa[...] * _mask_dot_lhs(dt, expand)
    hpg = M_HEADS // M_GROUPS
    hm = _head_masks()

    def chunk(c, carry):
        r0 = pl.multiple_of(c * CHUNK, CHUNK)
        rows = pl.ds(r0, CHUNK)
        xs = xa[rows, :]
        bcm = ba[rows, :]
        acum = ac_s[rows, :]
        lmat = lm_s[rows, :]
        a_last = acum[CHUNK - 1:CHUNK, :]
        e_cum = jnp.exp(acum)
        e_end = jnp.exp(a_last - acum)
        e_last = jnp.exp(a_last)
        xc = xc_s[rows, :]
        scores = []
        for g in range(M_GROUPS):
            b_g = bcm[:, g * M_STATE:(g + 1) * M_STATE]
            c_g = bcm[:, M_BC + g * M_STATE:M_BC + (g + 1) * M_STATE]
            scores.append(_dot(c_g, jnp.concatenate([b_g] * hpg, axis=0), NT))
        attn = jnp.concatenate(scores, axis=1) * lmat
        ys = []
        for pr in range(M_HEADS // 2):
            ps = slice(pr * LANES, (pr + 1) * LANES)
            g = (2 * pr) // hpg
            b_g = bcm[:, g * M_STATE:(g + 1) * M_STATE]
            c_g = bcm[:, M_BC + g * M_STATE:M_BC + (g + 1) * M_STATE]
            xc_p = xc[:, ps]
            st = st_ref[pr]
            y = _dot(attn[:, ps], _block_diag(xc_p, hm)) + _dot(c_g, st) * e_cum[:, ps]
            st_ref[pr] = st * e_last[:, ps] + _dot(b_g, xc_p * e_end[:, ps], TN)
            ys.append(y)
        y = (jnp.concatenate(ys, axis=1) + dsk_ref[...] * xs) * _silu(z_ref[rows, :].astype(F32))
        gw = M_WIDTH // M_GROUPS
        outs = []
        for g in range(M_GROUPS):
            yg = y[:, g * gw:(g + 1) * gw]
            outs.append(yg * lax.rsqrt(jnp.mean(yg * yg, axis=-1, keepdims=True) + NORM_EPS))
        o_ref[rows, :] = (jnp.concatenate(outs, axis=1) * nw_ref[...]).astype(o_ref.dtype)
        return carry

    lax.fori_loop(0, rb // CHUNK, chunk, 0)


def _mamba(p, conv_w, conv_b, dt_bias, a_log, d_skip, norm_w, batch, lp):
    rb = SEQ_BLOCK
    nb = lp // rb
    t = p.shape[0]

    def col(off, width):
        return pl.BlockSpec((rb, width), lambda b, j: (b * nb + j, off // width))

    def full(shape):
        return pl.BlockSpec(shape, lambda b, j: (0,) * len(shape))

    pad = LANES - M_HEADS
    dtb = jnp.pad(dt_bias.reshape(1, M_HEADS), ((0, 0), (0, pad)))
    alog = jnp.pad(a_log.reshape(1, M_HEADS), ((0, 0), (0, pad)))
    dsk = jnp.repeat(d_skip, M_HEAD_DIM).reshape(1, M_WIDTH)
    expand = (_iota((LANES, M_WIDTH), 1) // M_HEAD_DIM == _iota((LANES, M_WIDTH), 0)).astype(BF16)
    row_i = _iota((rb, M_WIDTH), 0) % CHUNK
    pos_j = _iota((rb, M_WIDTH), 1) % CHUNK
    consts = [expand, _chunk_tri(rb).astype(BF16), _chunk_tri(rb, lower=False).astype(BF16),
              (row_i <= pos_j).astype(F32), (row_i >= pos_j).astype(F32)]
    return pl.pallas_call(
        functools.partial(_mamba_kernel, rb=rb),
        grid=(batch, nb),
        in_specs=[col(C_Z, M_WIDTH), col(C_X, M_WIDTH), col(C_BC, 2 * M_BC), col(C_DT, LANES),
                  full((M_CONV, M_WIDTH)), full((M_CONV, 2 * M_BC)),
                  full((1, M_WIDTH)), full((1, 2 * M_BC)),
                  full((1, LANES)), full((1, LANES)), full((1, M_WIDTH)), full((1, M_WIDTH))]
        + [full(c.shape) for c in consts],
        out_specs=pl.BlockSpec((rb, M_WIDTH), lambda b, j: (b * nb + j, 0)),
        out_shape=jax.ShapeDtypeStruct((t, M_WIDTH), BF16),
        scratch_shapes=[pltpu.VMEM((M_HEADS // 2, M_STATE, LANES), F32),
                        pltpu.VMEM((rb + HIST, M_WIDTH), F32),
                        pltpu.VMEM((rb + HIST, 2 * M_BC), F32),
                        pltpu.VMEM((rb, M_WIDTH), F32),
                        pltpu.VMEM((rb, 2 * M_BC), F32)]
        + [pltpu.VMEM((rb, M_WIDTH), F32) for _ in range(3)],
        compiler_params=_params(("parallel", "arbitrary")),
        name="mamba2",
    )(p, p, p, p, conv_w[:, :M_WIDTH], conv_w[:, M_WIDTH:], conv_b[None, :M_WIDTH],
      conv_b[None, M_WIDTH:], dtb, alog, dsk, norm_w[None, :], *consts)


R_SHIFT_W = 3 * R_WIDTH
R_PRE = 2


def _rwkv_kernel(*refs, first, rb):
    if first:
        (rkv_ref, g1_ref, wa_ref, mu_rkv, mu_g1, mu_wa,
         w0_ref, w2_ref, a0_ref, a2_ref, g2_ref, kk_ref, ka_ref, rk_ref, gnw_ref, gnb_ref,
         o_ref, vf_out,
         st_ref, e_rkv, e_g1, e_wa, dec_s,
         lw_s, r_s, k_s, v_s, a_s, b_s, g_s, rt_s, wt_s, uv_s, arb_s, ov_s, pm_s, cm_s) = refs
    else:
        (rkv_ref, g1_ref, wa_ref, vr_ref, vf_ref, mu_rkv, mu_g1, mu_wa, mu_vr,
         w0_ref, w2_ref, a0_ref, a2_ref, g2_ref, kk_ref, ka_ref, rk_ref, gnw_ref, gnb_ref,
         v0_ref, v2_ref,
         o_ref,
         st_ref, e_rkv, e_g1, e_wa, e_vr, dec_s,
         lw_s, r_s, k_s, v_s, a_s, b_s, g_s, rt_s, wt_s, uv_s, arb_s, ov_s, pm_s, cm_s) = refs
    jb = pl.program_id(1)
    exts = [e_rkv, e_g1, e_wa] + ([] if first else [e_vr])

    @pl.when(jb == 0)
    def _():
        st_ref[...] = jnp.zeros(st_ref.shape, F32)
        for ext in exts:
            ext[0:HIST, :] = jnp.zeros((HIST, ext.shape[1]), F32)

    def shift(ext, src_ref, mu_ref):
        ext[HIST:HIST + rb, :] = src_ref[...].astype(F32)
        cur = ext[HIST:HIST + rb, :]
        prev = ext[HIST - 1:HIST - 1 + rb, :]
        out = cur + (prev - cur) * mu_ref[...]
        ext[0:HIST, :] = ext[rb:rb + HIST, :]
        return out

    rkv = shift(e_rkv, rkv_ref, mu_rkv)
    gl = shift(e_g1, g1_ref, mu_g1)
    wa = shift(e_wa, wa_ref, mu_wa)
    r = rkv[:, 0:R_WIDTH]
    k = rkv[:, R_WIDTH:2 * R_WIDTH]
    v = rkv[:, 2 * R_WIDTH:3 * R_WIDTH]
    wl = wa[:, 0:LANES]
    al = wa[:, LANES:2 * LANES]

    valid = ((jb * rb + _iota((rb, 1), 0)) >= N_PAD).astype(F32)
    w_log = -_softplus(-(w0_ref[...] + _dot(jnp.tanh(wl), w2_ref[...]))) - 0.5
    lw_s[...] = -jnp.exp(w_log)
    a = _sigmoid(a0_ref[...] + _dot(al, a2_ref[...]))
    if first:
        vf_out[...] = v
    else:
        vl = shift(e_vr, vr_ref, mu_vr)
        v = v + (vf_ref[...] - v) * _sigmoid(v0_ref[...] + _dot(vl, v2_ref[...]))
    g_s[...] = _dot(_sigmoid(gl), g2_ref[...])

    seg_ones = (_iota((LANES, LANES), 0) // R_HEAD_DIM == _iota((LANES, LANES), 1) // R_HEAD_DIM).astype(F32)

    def head_sum(x):
        return jnp.concatenate(
            [_mask_dot_lhs(x[:, s * LANES:(s + 1) * LANES], seg_ones, terms=2) for s in range(N_PAIRS)], axis=1)

    kk = k * kk_ref[...]
    kk = kk * lax.rsqrt(jnp.maximum(head_sum(kk * kk), L2_EPS))
    kh = k * (1.0 + (a - 1.0) * ka_ref[...]) * valid
    r_s[...] = r
    k_s[...] = kh
    v_s[...] = v
    a_s[...] = -kk
    b_s[...] = kk * a

    tri = _tri(CHUNK)
    t_i = _iota((CHUNK, LANES), 0)
    s_j = _iota((CHUNK, LANES), 1) % CHUNK
    strict = t_i > s_j
    incl = t_i >= s_j
    eye = (t_i == s_j).astype(F32)
    same_head = _iota((LANES, LANES), 0) // R_HEAD_DIM == _iota((LANES, LANES), 1) // R_HEAD_DIM
    n_double = 5
    assert 2 ** (n_double + 1) == CHUNK

    pairs = [slice(pr * LANES, (pr + 1) * LANES) for pr in range(N_PAIRS)]
    hm = _head_masks()

    def precompute(cc):
        units = []
        for sub in range(R_PRE):
            c = cc * R_PRE + sub
            rows = slice(c * CHUNK, (c + 1) * CHUNK)
            lw = lw_s[rows, :]
            cum = _mask_dot_rhs(tri, lw)
            c_last = cum[CHUNK - 1:CHUNK, :]
            e_inv = jnp.exp(-cum)
            e_end = jnp.exp(c_last - cum)
            b_c, k_c, v_c = b_s[rows, :], k_s[rows, :], v_s[rows, :]
            a_t = a_s[rows, :] * jnp.exp(cum - lw)
            r_t = r_s[rows, :] * jnp.exp(cum)
            b_t = b_c * e_inv
            k_t = k_c * e_inv
            b_h = b_c * e_end
            k_h = k_c * e_end
            rt_s[rows, :] = r_t
            dec_s[c:c + 1, :] = jnp.exp(c_last)
            units += [(rows, ps, a_t[:, ps], r_t[:, ps], b_t[:, ps], k_t[:, ps], v_c[:, ps],
                       b_h[:, ps], k_h[:, ps], c * N_PAIRS + pr) for pr, ps in enumerate(pairs)]
        m = [_dot(jnp.concatenate([a_t, r_t], axis=0),
                  jnp.concatenate([_block_diag(b_t, hm), _block_diag(k_t, hm)], axis=0), NT)
             for (_, _, a_t, r_t, b_t, k_t, _, _, _, _) in units]
        a_ab = [jnp.where(strict, x[0:CHUNK, 0:LANES], 0.0) for x in m]
        a_ak = [jnp.where(strict, x[0:CHUNK, LANES:2 * LANES], 0.0) for x in m]
        a_rk = [jnp.where(incl, x[CHUNK:2 * CHUNK, LANES:2 * LANES], 0.0) for x in m]
        for x, un in zip(m, units):
            arb_s[un[0], un[1]] = jnp.where(incl, x[CHUNK:2 * CHUNK, 0:LANES], 0.0)
        inv = [eye + x for x in a_ab]
        pw = [_dot(x, _block_diag(x, hm)) for x in a_ab]
        for it in range(n_double - 1):
            both = [_dot(jnp.concatenate([p_, i_], axis=0), _block_diag(p_, hm)) for p_, i_ in zip(pw, inv)]
            pw = [x[0:CHUNK] for x in both]
            inv = [i_ + x[CHUNK:2 * CHUNK] for i_, x in zip(inv, both)]
        inv = [i_ + _dot(i_, _block_diag(p_, hm)) for i_, p_ in zip(inv, pw)]
        t1ov = [_dot(jnp.concatenate([x, y], axis=0), _block_diag(un[6], hm))
                for x, y, un in zip(a_ak, a_rk, units)]
        wtuv = [_dot(i_, jnp.concatenate([_block_diag(un[2], hm), _block_diag(x[0:CHUNK], hm)], axis=1))
                for i_, un, x in zip(inv, units, t1ov)]
        wt = [x[:, 0:LANES] for x in wtuv]
        uv = [x[:, LANES:2 * LANES] for x in wtuv]
        pm = [_dot(w_, un[7], TN) for w_, un in zip(wt, units)]
        cm = [_dot(jnp.concatenate([x, un[6]], axis=0), jnp.concatenate([un[7], un[8]], axis=0), TN)
              for x, un in zip(uv, units)]
        for i, un in enumerate(units):
            wt_s[un[0], un[1]] = wt[i]
            uv_s[un[0], un[1]] = uv[i]
            ov_s[un[0], un[1]] = t1ov[i][CHUNK:2 * CHUNK]
            pm_s[un[9]] = jnp.where(same_head, pm[i], 0.0)
            cm_s[un[9]] = jnp.where(same_head, cm[i], 0.0)

    n_chunks = rb // CHUNK
    assert n_chunks % R_PRE == 0
    for cc in range(n_chunks // R_PRE):
        precompute(cc)

    inv_n = 1.0 / R_HEAD_DIM

    def finish(rows, o):
        mu = head_sum(o) * inv_n
        d = o - mu
        var = head_sum(d * d) * inv_n
        o = d * lax.rsqrt(var + R_GN_EPS) * gnw_ref[...] + gnb_ref[...]
        o = o + head_sum(r_s[rows, :] * k_s[rows, :] * rk_ref[...]) * v_s[rows, :]
        o_ref[rows, :] = (o * g_s[rows, :]).astype(o_ref.dtype)

    sts = [st_ref[pr] for pr in range(N_PAIRS)]
    pending = None
    for c in range(n_chunks):
        rows = slice(c * CHUNK, (c + 1) * CHUNK)
        dec = dec_s[c:c + 1, :]
        new_sts = [st * dec[:, ps] + _dot(st, pm_s[c * N_PAIRS + pr]) + cm_s[c * N_PAIRS + pr]
                   for pr, (ps, st) in enumerate(zip(pairs, sts))]
        uo = [_dot(jnp.concatenate([wt_s[rows, ps], rt_s[rows, ps]], axis=0), st, NT)
              for ps, st in zip(pairs, sts)]
        u = [x[0:CHUNK] + uv_s[rows, ps] for x, ps in zip(uo, pairs)]
        o1 = [x[CHUNK:2 * CHUNK] for x in uo]
        o2 = [_dot(arb_s[rows, ps], _block_diag(u_, hm)) for ps, u_ in zip(pairs, u)]
        sts = new_sts
        if pending is not None:
            finish(*pending)
        pending = (rows, jnp.concatenate(
            [o1[pr] + o2[pr] + ov_s[rows, ps] for pr, ps in enumerate(pairs)], axis=1))
    finish(*pending)
    for pr in range(N_PAIRS):
        st_ref[pr] = sts[pr]


def _rwkv(p, v_first, prm, first, batch, lp):
    rb = SEQ_BLOCK
    nb = lp // rb
    t = p.shape[0]

    def col(off, width):
        return pl.BlockSpec((rb, width), lambda b, j: (b * nb + j, off // width))

    def full(a):
        return pl.BlockSpec(a.shape, lambda b, j: (0,) * a.ndim)

    row_spec = pl.BlockSpec((rb, R_WIDTH), lambda b, j: (b * nb + j, 0))
    acts = [p, p, p]
    act_specs = [col(C_RKV, R_SHIFT_W), col(C_G1, R_GATE_RANK), col(C_WA, 2 * LANES)]
    mus = [prm["mu_rkv"], prm["mu_g1"], prm["mu_wa"]]
    tail = []
    if not first:
        acts += [p, v_first]
        act_specs += [col(C_VRES, LANES), row_spec]
        mus.append(prm["mu_vr"])
        tail = [prm["v0"], prm["v2"]]
    consts = mus + [prm[n] for n in ("w0", "w2", "a0", "a2", "g2", "k_k", "k_a", "r_k", "gn_w", "gn_b")] + tail
    out_shape = [jax.ShapeDtypeStruct((t, R_WIDTH), BF16)]
    out_specs = [row_spec]
    if first:
        out_shape.append(jax.ShapeDtypeStruct((t, R_WIDTH), F32))
        out_specs.append(row_spec)
    ext_w = [R_SHIFT_W, R_GATE_RANK, 2 * LANES] + ([] if first else [LANES])
    scratch = ([pltpu.VMEM((N_PAIRS, LANES, LANES), F32)]
               + [pltpu.VMEM((rb + HIST, w), F32) for w in ext_w]
               + [pltpu.VMEM((max(rb // CHUNK, 8), R_WIDTH), F32)]
               + [pltpu.VMEM((rb, R_WIDTH), F32) for _ in range(12)]
               + [pltpu.VMEM((rb // CHUNK * N_PAIRS, LANES, LANES), F32) for _ in range(2)])
    return pl.pallas_call(
        functools.partial(_rwkv_kernel, first=first, rb=rb),
        grid=(batch, nb),
        in_specs=act_specs + [full(a) for a in consts],
        out_specs=out_specs,
        out_shape=out_shape,
        scratch_shapes=scratch,
        compiler_params=_params(("parallel", "arbitrary")),
        name="rwkv7",
    )(*acts, *consts)


def _merge_kernel(oh_ref, om_ref, or_ref, g0_ref, g1_ref, g2_ref, wh_ref, wm_ref, wr_ref, o_ref):
    def gated(g_ref, x_ref, w_ref):
        return _sigmoid(g_ref[...].astype(F32)) * jnp.dot(x_ref[...], w_ref[...], preferred_element_type=F32)

    acc = gated(g0_ref, oh_ref, wh_ref) + gated(g1_ref, om_ref, wm_ref) + gated(g2_ref, or_ref, wr_ref)
    o_ref[...] = acc.astype(o_ref.dtype)


def _merge(o_h, o_m, o_r, p, w_h, w_m, w_r, tm, tn):
    t = p.shape[0]
    gb = C_GATE // tn
    nt = D_MODEL // tn

    def act():
        return pl.BlockSpec((tm, H_WIDTH), lambda i, j: (i, 0))

    def gate(k):
        return pl.BlockSpec((tm, tn), lambda i, j, k=k: (i, gb + k * nt + j))

    def wt():
        return pl.BlockSpec((H_WIDTH, tn), lambda i, j: (0, j))

    return pl.pallas_call(
        _merge_kernel,
        grid=(t // tm, nt),
        in_specs=[act(), act(), act(), gate(0), gate(1), gate(2), wt(), wt(), wt()],
        out_specs=pl.BlockSpec((tm, tn), lambda i, j: (i, j)),
        out_shape=jax.ShapeDtypeStruct((t, D_MODEL), BF16),
        compiler_params=_params(("parallel", "arbitrary")),
        name="merge",
    )(o_h, o_m, o_r, p, p, p, w_h, w_m, w_r)


def _row_valid(tm, lp):
    pos = (pl.program_id(0) * tm + _iota((tm, 1), 0)) % lp
    return (pos >= N_PAD).astype(F32)


def _tail_kernel(h_ref, m_ref, wo_ref, lnw_ref, w1_ref, w2_ref, lnf_ref, o_ref, u_ref, *, tm, lp, final):
    kk = pl.program_id(1)

    @pl.when(kk == 0)
    def _():
        valid = _row_valid(tm, lp)
        mixed = jnp.where(valid > 0.0, m_ref[...], jnp.zeros((), m_ref.dtype))
        x = h_ref[...] + jnp.dot(mixed, wo_ref[...], preferred_element_type=F32)
        ms = jnp.mean(x * x, axis=-1, keepdims=True)
        u_ref[...] = (x * lax.rsqrt(ms + NORM_EPS) * lnw_ref[...] * valid).astype(BF16)
        o_ref[...] = x

    a = jnp.maximum(jnp.dot(u_ref[...], w1_ref[...], preferred_element_type=F32), 0.0)
    o_ref[...] += jnp.dot((a * a).astype(BF16), w2_ref[...], preferred_element_type=F32)

    if final:
        @pl.when(kk == pl.num_programs(1) - 1)
        def _():
            y = o_ref[...]
            ms = jnp.mean(y * y, axis=-1, keepdims=True)
            o_ref[...] = y * lax.rsqrt(ms + NORM_EPS) * lnf_ref[...]


def _tail(h, mixed, w_out, lnw, w1, w2, lnf, tm, th, lp, final):
    t = h.shape[0]
    return pl.pallas_call(
        functools.partial(_tail_kernel, tm=tm, lp=lp, final=final),
        grid=(t // tm, MLP_HIDDEN // th),
        in_specs=[pl.BlockSpec((tm, D_MODEL), lambda i, k: (i, 0)),
                  pl.BlockSpec((tm, D_MODEL), lambda i, k: (i, 0)),
                  pl.BlockSpec((D_MODEL, D_MODEL), lambda i, k: (0, 0), pipeline_mode=pl.Buffered(1)),
                  pl.BlockSpec((1, D_MODEL), lambda i, k: (0, 0)),
                  pl.BlockSpec((D_MODEL, th), lambda i, k: (0, k)),
                  pl.BlockSpec((th, D_MODEL), lambda i, k: (k, 0)),
                  pl.BlockSpec((1, D_MODEL), lambda i, k: (0, 0))],
        out_specs=pl.BlockSpec((tm, D_MODEL), lambda i, k: (i, 0)),
        out_shape=jax.ShapeDtypeStruct((t, D_MODEL), F32),
        scratch_shapes=[pltpu.VMEM((tm, D_MODEL), BF16)],
        compiler_params=_params(("parallel", "arbitrary")),
        name="out_mlp",
    )(h, mixed, w_out, lnw, w1, w2, lnf)


def _pad_cols(a, width):
    return jnp.pad(a, ((0, 0), (0, width - a.shape[1])))


def _pad_rows(a, height):
    return jnp.pad(a, ((0, height - a.shape[0]), (0, 0)))


def _split_in_cols(a, a_vres):
    o = 0
    hcols = a[:, o:o + H_COLS]; o += H_COLS
    z = a[:, o:o + M_WIDTH]; o += M_WIDTH
    xm = a[:, o:o + M_WIDTH]; o += M_WIDTH
    bc = a[:, o:o + 2 * M_BC]; o += 2 * M_BC
    dt = a[:, o:o + M_HEADS]; o += M_HEADS
    rkv = a[:, o:o + 3 * R_WIDTH]; o += 3 * R_WIDTH
    w1 = a[:, o:o + R_DECAY_RANK]; o += R_DECAY_RANK
    a1 = a[:, o:o + R_AAA_RANK]; o += R_AAA_RANK
    g1 = a[:, o:o + R_GATE_RANK]; o += R_GATE_RANK
    gates = a[:, o:o + GATE_COLS]; o += GATE_COLS
    assert o == IN_COLS
    out = jnp.concatenate([hcols, gates, z, xm, rkv, bc, g1, _pad_cols(w1, LANES), _pad_cols(a1, LANES),
                           _pad_cols(dt, LANES), _pad_cols(a_vres, LANES)], axis=1)
    assert out.shape[1] == P_COLS
    return out


def _tile(t, candidates):
    for c in candidates:
        if t % c == 0:
            return c
    raise ValueError(f"no tile for {t}")


def kernel(x, meta, ln1_w, ln2_w, lnf_w, w_in, w_in_vres, hg_lb_logits, hg_norm_w, m_conv_w, m_conv_b, m_dt_bias, m_a_log, m_d, m_norm_w, r_mu, r_mu_vres, r_w0, r_w2, r_a0, r_a2, r_v0, r_v2, r_g2, r_k_k, r_k_a, r_r_k, r_gn_w, r_gn_b, w_up_h, w_up_m, w_up_r, w_out, w_mlp_in, w_mlp_out):
    batch, seq, d = x.shape
    assert d == D_MODEL and M_HEAD_DIM == CHUNK and R_HEAD_DIM == CHUNK
    l_real = seq + CHUNK
    lp = -(-l_real // ROW_ALIGN) * ROW_ALIGN
    t = batch * lp
    tm_proj = _tile(t, (1280, 640))
    tm = _tile(t, (640,))
    tm_tail = _tile(t, (512, 256))

    h = jnp.concatenate([jnp.zeros((batch, N_PAD, d), x.dtype),
                         jnp.broadcast_to(meta.astype(x.dtype), (batch, N_META, d)),
                         x, jnp.zeros((batch, lp - l_real, d), x.dtype)], axis=1).reshape(t, d)

    v_first = None
    for l in range(DEPTH):
        vres_w = w_in_vres[l - 1] if l > 0 else jnp.zeros((d, R_MV_RANK), w_in.dtype)
        w_comb = _split_in_cols(w_in[l], vres_w).astype(BF16)
        p = _proj(h, ln1_w[l][None, :], w_comb, tm_proj, 1280)

        o_h = _hgrn(p, hg_lb_logits, hg_norm_w[l].reshape(1, H_WIDTH), l, batch, lp)
        o_m = _mamba(p, m_conv_w[l], m_conv_b[l], m_dt_bias[l], m_a_log[l], m_d[l], m_norm_w[l], batch, lp)

        mu = r_mu[l][None, :]
        o = 3 * R_WIDTH
        prm = {
            "mu_rkv": mu[:, :o],
            "mu_wa": jnp.concatenate([_pad_cols(mu[:, o:o + R_DECAY_RANK], LANES),
                                      _pad_cols(mu[:, o + R_DECAY_RANK:o + R_DECAY_RANK + R_AAA_RANK], LANES)], axis=1),
            "mu_g1": mu[:, o + R_DECAY_RANK + R_AAA_RANK:],
            "w0": r_w0[l][None, :], "w2": _pad_rows(r_w2[l], LANES).astype(BF16),
            "a0": r_a0[l][None, :], "a2": _pad_rows(r_a2[l], LANES).astype(BF16),
            "g2": r_g2[l].astype(BF16),
            "k_k": r_k_k[l][None, :], "k_a": r_k_a[l][None, :], "r_k": r_r_k[l].reshape(1, R_WIDTH),
            "gn_w": r_gn_w[l][None, :], "gn_b": r_gn_b[l][None, :],
        }
        if l == 0:
            o_r, v_first = _rwkv(p, None, prm, True, batch, lp)
        else:
            prm["mu_vr"] = _pad_cols(r_mu_vres[l - 1][None, :], LANES)
            prm["v0"] = r_v0[l - 1][None, :]
            prm["v2"] = _pad_rows(r_v2[l - 1], LANES).astype(BF16)
            (o_r,) = _rwkv(p, v_first, prm, False, batch, lp)

        mixed = _merge(o_h, o_m, o_r, p, w_up_h[l].astype(BF16), w_up_m[l].astype(BF16),
                       w_up_r[l].astype(BF16), tm, 1024)
        h = _tail(h, mixed, w_out[l].astype(BF16), ln2_w[l][None, :], w_mlp_in[l].astype(BF16),
                  w_mlp_out[l].astype(BF16), lnf_w[None, :], tm_tail, 1024, lp, l == DEPTH - 1)

    return h.reshape(batch, lp, d)[:, CHUNK:l_real]
```

```python
import functools

import jax
import jax.numpy as jnp
from jax import lax
from jax.experimental import pallas as pl
from jax.experimental.pallas import tpu as pltpu

F32 = jnp.float32
BF16 = jnp.bfloat16

LANES = 128

D_MODEL = 2048
DEPTH = 2
N_META = 16
CHUNK = 64
N_PAD = CHUNK - N_META
MLP_HIDDEN = 4 * D_MODEL
NORM_EPS = 1e-6
N_BRANCH = 3
L2_EPS = 1e-24

H_WIDTH = D_MODEL // 2
H_HEAD_DIM = 128
H_HEADS = H_WIDTH // H_HEAD_DIM

M_WIDTH = D_MODEL // 2
M_HEAD_DIM = 64
M_HEADS = M_WIDTH // M_HEAD_DIM
M_GROUPS = 2
M_STATE = 128
M_CONV = 4
M_BC = M_GROUPS * M_STATE

R_WIDTH = D_MODEL // 2
R_HEAD_DIM = 64
R_HEADS = R_WIDTH // R_HEAD_DIM
R_DECAY_RANK = max(32, int(round(1.8 * D_MODEL ** 0.5 / 32)) * 32)
R_AAA_RANK = max(32, int(round(1.8 * D_MODEL ** 0.5 / 32)) * 32)
R_MV_RANK = max(32, int(round(1.3 * D_MODEL ** 0.5 / 32)) * 32)
R_GATE_RANK = max(32, int(round(0.6 * D_MODEL ** 0.8 / 32)) * 32)
R_GN_EPS = 64e-5
N_PAIRS = R_HEADS // 2

H_COLS = 4 * H_WIDTH
M_CONV_CH = M_WIDTH + 2 * M_BC
M_COLS = M_WIDTH + M_CONV_CH + M_HEADS
R_COLS = 3 * R_WIDTH + R_DECAY_RANK + R_AAA_RANK + R_GATE_RANK
GATE_COLS = N_BRANCH * D_MODEL
IN_COLS = H_COLS + M_COLS + R_COLS + GATE_COLS

C_H = 0
C_GATE = C_H + H_COLS
C_Z = C_GATE + GATE_COLS
C_X = C_Z + M_WIDTH
C_RKV = C_X + M_WIDTH
C_BC = C_RKV + 3 * R_WIDTH
C_G1 = C_BC + 2 * M_BC
C_WA = C_G1 + R_GATE_RANK
C_DT = C_WA + 2 * LANES
C_VRES = C_DT + LANES
P_COLS = C_VRES + LANES

ROW_ALIGN = 1280
SEQ_BLOCK = 256
HG_BLOCK = 640
VMEM_LIMIT = 56 * 1024 * 1024

NN = (((1,), (0,)), ((), ()))
NT = (((1,), (1,)), ((), ()))
TN = (((0,), (0,)), ((), ()))


def _dot(a, b, dims=NN):
    return lax.dot_general(a.astype(BF16), b.astype(BF16), dims, preferred_element_type=F32)


def _split(x, terms):
    out = []
    for _ in range(terms - 1):
        hi = x.astype(BF16)
        out.append(hi)
        x = x - hi.astype(F32)
    out.append(x.astype(BF16))
    return out


def _mask_dot_rhs(m, x, terms=3):
    mb = m.astype(BF16)
    return jnp.dot(jnp.concatenate([mb] * terms, axis=1), jnp.concatenate(_split(x, terms), axis=0),
                   preferred_element_type=F32)


def _mask_dot_lhs(x, m, terms=3):
    mb = m.astype(BF16)
    return jnp.dot(jnp.concatenate(_split(x, terms), axis=1), jnp.concatenate([mb] * terms, axis=0),
                   preferred_element_type=F32)


LOG2E = 1.4426950408889634


def _sigmoid(x):
    return 1.0 / (1.0 + jnp.exp2(x * -LOG2E))


def _silu(x):
    return x * _sigmoid(x)


def _softplus(x):
    return jnp.maximum(x, 0.0) + jnp.log(1.0 + jnp.exp(-jnp.abs(x)))


def _iota(shape, dim):
    return lax.broadcasted_iota(jnp.int32, shape, dim)


def _tri(n):
    return (_iota((n, n), 0) >= _iota((n, n), 1)).astype(F32)


def _head_masks():
    lane = _iota((CHUNK, LANES), 1)
    return lane < 64, lane >= 64


def _block_diag(x, masks):
    return jnp.concatenate([jnp.where(masks[0], x, 0.0), jnp.where(masks[1], x, 0.0)], axis=0)


def _params(sem):
    return pltpu.CompilerParams(dimension_semantics=sem, vmem_limit_bytes=VMEM_LIMIT)


def _proj_kernel(h_ref, lnw_ref, w_ref, o_ref, u_ref):
    @pl.when(pl.program_id(1) == 0)
    def _():
        x = h_ref[...]
        ms = jnp.mean(x * x, axis=-1, keepdims=True)
        u_ref[...] = (x * lax.rsqrt(ms + NORM_EPS) * lnw_ref[...]).astype(BF16)

    o_ref[...] = jnp.dot(u_ref[...], w_ref[...], preferred_element_type=F32).astype(o_ref.dtype)


def _proj(h, lnw, w, tm, tn):
    t = h.shape[0]
    return pl.pallas_call(
        _proj_kernel,
        grid=(t // tm, P_COLS // tn),
        in_specs=[
            pl.BlockSpec((tm, D_MODEL), lambda i, j: (i, 0)),
            pl.BlockSpec((1, D_MODEL), lambda i, j: (0, 0)),
            pl.BlockSpec((D_MODEL, tn), lambda i, j: (0, j)),
        ],
        out_specs=pl.BlockSpec((tm, tn), lambda i, j: (i, j)),
        out_shape=jax.ShapeDtypeStruct((t, P_COLS), BF16),
        scratch_shapes=[pltpu.VMEM((tm, D_MODEL), BF16)],
        compiler_params=_params(("parallel", "arbitrary")),
        name="in_proj",
    )(h, lnw, w)


HG_SUB = 8
HG_UNROLL = 2


def _hgrn_kernel(q_ref, f_ref, i_ref, g_ref, lbl_ref, nw_ref, o_ref,
                 st_ref, *, layer, rb):
    jb = pl.program_id(1)

    @pl.when(jb == 0)
    def _():
        st_ref[...] = jnp.zeros(st_ref.shape, F32)

    lg = lbl_ref[...]
    mx = jnp.max(lg, axis=0, keepdims=True)
    ex = jnp.exp(lg - mx)
    sm = ex / jnp.sum(ex, axis=0, keepdims=True)
    lb = jnp.sum(sm[0:layer + 1], axis=0, keepdims=True) - sm[0:1]

    assert HG_SUB == 8
    col8 = _iota((HG_SUB, H_HEAD_DIM), 1)
    key_row = [(_iota((HG_SUB, H_HEAD_DIM), 0) + (HG_SUB - d)) % HG_SUB for d in range(HG_SUB)]
    heads = [slice(hd * H_HEAD_DIM, (hd + 1) * H_HEAD_DIM) for hd in range(H_HEADS)]
    arow = _iota((CHUNK, CHUNK), 0)
    acol = _iota((CHUNK, CHUNK), 1)
    off_mask = acol < (arow // HG_SUB) * HG_SUB
    causal = arow >= acol
    n_sub = CHUNK // HG_SUB

    tri = _tri(CHUNK)
    row = _iota((CHUNK, 1), 0)

    def one_chunk(r0):
        rows = pl.ds(r0, CHUNK)
        valid = ((jb * rb + r0 + row) >= N_PAD).astype(F32)
        q = _silu(q_ref[rows, :].astype(F32))
        fr = f_ref[rows, :].astype(F32)
        sig = _sigmoid(fr)
        k = (1.0 - lb) * (1.0 - sig) * valid
        v = i_ref[rows, :].astype(F32)
        g2 = _mask_dot_rhs(tri, jnp.log(lb + (1.0 - lb) * sig)) * LOG2E
        hk = g2 - jnp.log(k) * LOG2E

        diag = [[None] * n_sub for _ in heads]
        for sub in range(n_sub):
            base = sub * HG_SUB
            sub_rows = slice(base, base + HG_SUB)
            g_i, q_i, k_i, h_i = g2[sub_rows], q[sub_rows], k[sub_rows], hk[sub_rows]
            acc = [jnp.zeros((HG_SUB, H_HEAD_DIM), F32) for _ in heads]
            for d in range(HG_SUB):
                if d == 0:
                    e = q_i * k_i
                else:
                    e = jnp.exp2(jnp.minimum(g_i - pltpu.roll(h_i, d, axis=0), 0.0)) * q_i
                at_key = col8 == key_row[d] + base
                for hd, hs in enumerate(heads):
                    s = jnp.sum(e[:, hs], axis=-1, keepdims=True)
                    acc[hd] = jnp.where(at_key, s, acc[hd])
            for hd in range(H_HEADS):
                diag[hd][sub] = acc[hd]

        g_last = g2[CHUNK - 1:CHUNK, :]
        qg = q * jnp.exp2(g2)
        kdec = jnp.exp2(g_last - hk)
        e_last = jnp.exp2(g_last)
        attn = []
        for hd, hs in enumerate(heads):
            g_h, q_h, hk_h = g2[:, hs], q[:, hs], hk[:, hs]
            qcat, kcat = [], []
            for sub in range(1, n_sub):
                base = sub * HG_SUB
                g_r = g_h[base:base + 1, :]
                q_sub = q_h[base:base + HG_SUB] * jnp.exp2(jnp.minimum(g_h[base:base + HG_SUB] - g_r, 0.0))
                pieces = [jnp.zeros((base, H_HEAD_DIM), F32), q_sub]
                if base + HG_SUB < CHUNK:
                    pieces.append(jnp.zeros((CHUNK - base - HG_SUB, H_HEAD_DIM), F32))
                qcat.append(jnp.concatenate(pieces, axis=0))
                k_sub = jnp.exp2(jnp.minimum(g_r - hk_h[0:base], 0.0))
                kcat.append(jnp.concatenate([k_sub, jnp.zeros((CHUNK - base, H_HEAD_DIM), F32)], axis=0))
            a_off = _dot(jnp.concatenate(qcat, axis=1), jnp.concatenate(kcat, axis=1), NT)
            a_diag = jnp.concatenate(diag[hd], axis=0)[:, 0:CHUNK]
            attn.append(jnp.where(off_mask, a_off, jnp.where(causal, a_diag, 0.0)))
        sts = [st_ref[hd] for hd in range(H_HEADS)]
        inter = [_dot(qg[:, hs], sts[hd], NT) for hd, hs in enumerate(heads)]
        upd = [_dot(v[:, hs], kdec[:, hs], TN) for hs in heads]
        intra = [_dot(attn[hd], v[:, hs]) for hd, hs in enumerate(heads)]
        outs = []
        for hd, hs in enumerate(heads):
            st_ref[hd] = sts[hd] * e_last[:, hs] + upd[hd]
            o_h = (intra[hd] + inter[hd]) * _sigmoid(g_ref[rows, hs].astype(F32))
            outs.append(o_h * lax.rsqrt(jnp.mean(o_h * o_h, axis=-1, keepdims=True) + NORM_EPS))
        o_ref[rows, :] = (jnp.concatenate(outs, axis=-1) * nw_ref[...]).astype(o_ref.dtype)

    def chunks(cc, carry):
        for u in range(HG_UNROLL):
            one_chunk(pl.multiple_of((cc * HG_UNROLL + u) * CHUNK, CHUNK))
        return carry

    assert (rb // CHUNK) % HG_UNROLL == 0
    lax.fori_loop(0, rb // CHUNK // HG_UNROLL, chunks, 0)


def _hgrn(p, lb_logits, norm_w, layer, batch, lp):
    rb = HG_BLOCK
    nb = lp // rb
    t = p.shape[0]
    cb = C_H // H_WIDTH

    def col(k):
        return pl.BlockSpec((rb, H_WIDTH), lambda b, j, k=k: (b * nb + j, cb + k))

    return pl.pallas_call(
        functools.partial(_hgrn_kernel, layer=layer, rb=rb),
        grid=(batch, nb),
        in_specs=[col(0), col(1), col(2), col(3),
                  pl.BlockSpec((DEPTH, H_WIDTH), lambda b, j: (0, 0)),
                  pl.BlockSpec((1, H_WIDTH), lambda b, j: (0, 0))],
        out_specs=pl.BlockSpec((rb, H_WIDTH), lambda b, j: (b * nb + j, 0)),
        out_shape=jax.ShapeDtypeStruct((t, H_WIDTH), BF16),
        scratch_shapes=[pltpu.VMEM((H_HEADS, H_HEAD_DIM, H_HEAD_DIM), F32)],
        compiler_params=_params(("parallel", "arbitrary")),
        name="hgrn2",
    )(p, p, p, p, lb_logits, norm_w)


HIST = 8


def _mamba_kernel(z_ref, x_ref, bc_ref, dt_ref, cwx_ref, cwb_ref, cbx_ref, cbb_ref,
                  dtb_ref, alog_ref, dsk_ref, nw_ref, expand_ref, tri_ref, ones_ref, le_ref, ge_ref, o_ref,
                  st_ref, xe, be, xa, ba, xc_s, da_s, *, rb):
    jb = pl.program_id(1)

    @pl.when(jb == 0)
    def _():
        st_ref[...] = jnp.zeros(st_ref.shape, F32)
        xe[0:HIST, :] = jnp.zeros((HIST, M_WIDTH), F32)
        be[0:HIST, :] = jnp.zeros((HIST, 2 * M_BC), F32)

    def conv(ext, src_ref, w_ref, b_ref, dst):
        ext[HIST:HIST + rb, :] = src_ref[...].astype(F32)
        acc = b_ref[...]
        for tap in range(M_CONV):
            o = HIST - (M_CONV - 1) + tap
            acc = acc + w_ref[tap:tap + 1, :] * ext[o:o + rb, :]
        dst[...] = _silu(acc)
        ext[0:HIST, :] = ext[rb:rb + HIST, :]

    conv(xe, x_ref, cwx_ref, cbx_ref, xa)
    conv(be, bc_ref, cwb_ref, cbb_ref, ba)

    expand = expand_ref[...]
    valid = ((jb * rb + _iota((rb, 1), 0)) >= N_PAD).astype(F32)
    dt = _softplus(dt_ref[...].astype(F32) + dtb_ref[...]) * valid
    da_s[...] = _mask_dot_lhs(dt * -jnp.exp(alog_ref[...]), expand)
    xc_s[...] = xa[...] * _mask_dot_lhs(dt, expand, terms=2)
    hpg = M_HEADS // M_GROUPS
    hm = _head_masks()

    sts = [st_ref[pr] for pr in range(M_HEADS // 2)]
    for c in range(rb // CHUNK):
        rows = slice(c * CHUNK, (c + 1) * CHUNK)
        xs = xa[rows, :]
        bcm = ba[rows, :]
        da_e = da_s[rows, :]
        acum = _mask_dot_rhs(tri_ref[...], da_e)
        acum_j = _mask_dot_rhs(ones_ref[...], da_e * le_ref[...])
        lmat = jnp.exp(jnp.minimum(acum - acum_j, 0.0)) * ge_ref[...]
        a_last = acum[CHUNK - 1:CHUNK, :]
        e_cum = jnp.exp(acum)
        e_end = jnp.exp(a_last - acum)
        e_last = jnp.exp(a_last)
        xc = xc_s[rows, :]
        scores = []
        for g in range(M_GROUPS):
            b_g = bcm[:, g * M_STATE:(g + 1) * M_STATE]
            c_g = bcm[:, M_BC + g * M_STATE:M_BC + (g + 1) * M_STATE]
            scores.append(_dot(c_g, jnp.concatenate([b_g] * hpg, axis=0), NT))
        attn = jnp.concatenate(scores, axis=1) * lmat
        ys = []
        for pr in range(M_HEADS // 2):
            ps = slice(pr * LANES, (pr + 1) * LANES)
            g = (2 * pr) // hpg
            b_g = bcm[:, g * M_STATE:(g + 1) * M_STATE]
            c_g = bcm[:, M_BC + g * M_STATE:M_BC + (g + 1) * M_STATE]
            xc_p = xc[:, ps]
            y = _dot(attn[:, ps], _block_diag(xc_p, hm)) + _dot(c_g, sts[pr]) * e_cum[:, ps]
            sts[pr] = sts[pr] * e_last[:, ps] + _dot(b_g, xc_p * e_end[:, ps], TN)
            ys.append(y)
        y = (jnp.concatenate(ys, axis=1) + dsk_ref[...] * xs) * _silu(z_ref[rows, :].astype(F32))
        gw = M_WIDTH // M_GROUPS
        outs = []
        for g in range(M_GROUPS):
            yg = y[:, g * gw:(g + 1) * gw]
            outs.append(yg * lax.rsqrt(jnp.mean(yg * yg, axis=-1, keepdims=True) + NORM_EPS))
        o_ref[rows, :] = (jnp.concatenate(outs, axis=1) * nw_ref[...]).astype(o_ref.dtype)
    for pr in range(M_HEADS // 2):
        st_ref[pr] = sts[pr]


def _mamba(p, conv_w, conv_b, dt_bias, a_log, d_skip, norm_w, batch, lp):
    rb = SEQ_BLOCK
    nb = lp // rb
    t = p.shape[0]

    def col(off, width):
        return pl.BlockSpec((rb, width), lambda b, j: (b * nb + j, off // width))

    def full(shape):
        return pl.BlockSpec(shape, lambda b, j: (0,) * len(shape))

    pad = LANES - M_HEADS
    dtb = jnp.pad(dt_bias.reshape(1, M_HEADS), ((0, 0), (0, pad)))
    alog = jnp.pad(a_log.reshape(1, M_HEADS), ((0, 0), (0, pad)))
    dsk = jnp.repeat(d_skip, M_HEAD_DIM).reshape(1, M_WIDTH)
    expand = (_iota((LANES, M_WIDTH), 1) // M_HEAD_DIM == _iota((LANES, M_WIDTH), 0)).astype(BF16)
    row_i = _iota((CHUNK, M_WIDTH), 0)
    pos_j = _iota((CHUNK, M_WIDTH), 1) % CHUNK
    consts = [expand, _tri(CHUNK).astype(BF16), jnp.ones((CHUNK, CHUNK), BF16),
              (row_i <= pos_j).astype(F32), (row_i >= pos_j).astype(F32)]
    return pl.pallas_call(
        functools.partial(_mamba_kernel, rb=rb),
        grid=(batch, nb),
        in_specs=[col(C_Z, M_WIDTH), col(C_X, M_WIDTH), col(C_BC, 2 * M_BC), col(C_DT, LANES),
                  full((M_CONV, M_WIDTH)), full((M_CONV, 2 * M_BC)),
                  full((1, M_WIDTH)), full((1, 2 * M_BC)),
                  full((1, LANES)), full((1, LANES)), full((1, M_WIDTH)), full((1, M_WIDTH))]
        + [full(c.shape) for c in consts],
        out_specs=pl.BlockSpec((rb, M_WIDTH), lambda b, j: (b * nb + j, 0)),
        out_shape=jax.ShapeDtypeStruct((t, M_WIDTH), BF16),
        scratch_shapes=[pltpu.VMEM((M_HEADS // 2, M_STATE, LANES), F32),
                        pltpu.VMEM((rb + HIST, M_WIDTH), F32),
                        pltpu.VMEM((rb + HIST, 2 * M_BC), F32),
                        pltpu.VMEM((rb, M_WIDTH), F32),
                        pltpu.VMEM((rb, 2 * M_BC), F32)]
        + [pltpu.VMEM((rb, M_WIDTH), F32) for _ in range(2)],
        compiler_params=_params(("parallel", "arbitrary")),
        name="mamba2",
    )(p, p, p, p, conv_w[:, :M_WIDTH], conv_w[:, M_WIDTH:], conv_b[None, :M_WIDTH],
      conv_b[None, M_WIDTH:], dtb, alog, dsk, norm_w[None, :], *consts)


R_SHIFT_W = 3 * R_WIDTH
R_PRE = 2


def _rwkv_kernel(*refs, first, rb):
    if first:
        (rkv_ref, g1_ref, wa_ref, mu_rkv, mu_g1, mu_wa,
         w0_ref, w2_ref, a0_ref, a2_ref, g2_ref, kk_ref, ka_ref, rk_ref, gnw_ref, gnb_ref,
         o_ref, vf_out,
         st_ref, e_rkv, e_g1, e_wa, dec_s,
         lw_s, r_s, k_s, v_s, a_s, b_s, g_s, rt_s, wt_s, uv_s, arb_s, ov_s, pm_s, cm_s) = refs
    else:
        (rkv_ref, g1_ref, wa_ref, vr_ref, vf_ref, mu_rkv, mu_g1, mu_wa, mu_vr,
         w0_ref, w2_ref, a0_ref, a2_ref, g2_ref, kk_ref, ka_ref, rk_ref, gnw_ref, gnb_ref,
         v0_ref, v2_ref,
         o_ref,
         st_ref, e_rkv, e_g1, e_wa, e_vr, dec_s,
         lw_s, r_s, k_s, v_s, a_s, b_s, g_s, rt_s, wt_s, uv_s, arb_s, ov_s, pm_s, cm_s) = refs
    jb = pl.program_id(1)
    exts = [e_rkv, e_g1, e_wa] + ([] if first else [e_vr])

    @pl.when(jb == 0)
    def _():
        st_ref[...] = jnp.zeros(st_ref.shape, F32)
        for ext in exts:
            ext[0:HIST, :] = jnp.zeros((HIST, ext.shape[1]), F32)

    def shift(ext, src_ref, mu_ref):
        ext[HIST:HIST + rb, :] = src_ref[...].astype(F32)
        cur = ext[HIST:HIST + rb, :]
        prev = ext[HIST - 1:HIST - 1 + rb, :]
        out = cur + (prev - cur) * mu_ref[...]
        ext[0:HIST, :] = ext[rb:rb + HIST, :]
        return out

    rkv = shift(e_rkv, rkv_ref, mu_rkv)
    gl = shift(e_g1, g1_ref, mu_g1)
    wa = shift(e_wa, wa_ref, mu_wa)
    r = rkv[:, 0:R_WIDTH]
    k = rkv[:, R_WIDTH:2 * R_WIDTH]
    v = rkv[:, 2 * R_WIDTH:3 * R_WIDTH]
    wl = wa[:, 0:LANES]
    al = wa[:, LANES:2 * LANES]

    valid = ((jb * rb + _iota((rb, 1), 0)) >= N_PAD).astype(F32)
    w_log = -_softplus(-(w0_ref[...] + _dot(jnp.tanh(wl), w2_ref[...]))) - 0.5
    lw_s[...] = -jnp.exp(w_log)
    a = _sigmoid(a0_ref[...] + _dot(al, a2_ref[...]))
    if first:
        vf_out[...] = v
    else:
        vl = shift(e_vr, vr_ref, mu_vr)
        v = v + (vf_ref[...] - v) * _sigmoid(v0_ref[...] + _dot(vl, v2_ref[...]))
    g_s[...] = _dot(_sigmoid(gl), g2_ref[...])

    seg_ones = (_iota((LANES, LANES), 0) // R_HEAD_DIM == _iota((LANES, LANES), 1) // R_HEAD_DIM).astype(F32)

    def head_sum(x):
        return jnp.concatenate(
            [_mask_dot_lhs(x[:, s * LANES:(s + 1) * LANES], seg_ones, terms=2) for s in range(N_PAIRS)], axis=1)

    kk = k * kk_ref[...]
    kk = kk * lax.rsqrt(jnp.maximum(head_sum(kk * kk), L2_EPS))
    kh = k * (1.0 + (a - 1.0) * ka_ref[...]) * valid
    r_s[...] = r
    k_s[...] = kh
    v_s[...] = v
    a_s[...] = -kk
    b_s[...] = kk * a

    tri = _tri(CHUNK)
    t_i = _iota((CHUNK, LANES), 0)
    s_j = _iota((CHUNK, LANES), 1) % CHUNK
    strict = t_i > s_j
    incl = t_i >= s_j
    eye = (t_i == s_j).astype(F32)
    same_head = _iota((LANES, LANES), 0) // R_HEAD_DIM == _iota((LANES, LANES), 1) // R_HEAD_DIM
    n_double = 5
    assert 2 ** (n_double + 1) == CHUNK

    pairs = [slice(pr * LANES, (pr + 1) * LANES) for pr in range(N_PAIRS)]
    hm = _head_masks()

    def precompute(cc):
        units = []
        for sub in range(R_PRE):
            c = cc * R_PRE + sub
            rows = slice(c * CHUNK, (c + 1) * CHUNK)
            lw = lw_s[rows, :]
            cum = _mask_dot_rhs(tri, lw)
            c_last = cum[CHUNK - 1:CHUNK, :]
            e_inv = jnp.exp(-cum)
            e_end = jnp.exp(c_last - cum)
            b_c, k_c, v_c = b_s[rows, :], k_s[rows, :], v_s[rows, :]
            a_t = a_s[rows, :] * jnp.exp(cum - lw)
            r_t = r_s[rows, :] * jnp.exp(cum)
            b_t = b_c * e_inv
            k_t = k_c * e_inv
            b_h = b_c * e_end
            k_h = k_c * e_end
            rt_s[rows, :] = r_t
            dec_s[c:c + 1, :] = jnp.exp(c_last)
            units += [(rows, ps, a_t[:, ps], r_t[:, ps], b_t[:, ps], k_t[:, ps], v_c[:, ps],
                       b_h[:, ps], k_h[:, ps], c * N_PAIRS + pr) for pr, ps in enumerate(pairs)]
        m = [_dot(jnp.concatenate([a_t, r_t], axis=0),
                  jnp.concatenate([_block_diag(b_t, hm), _block_diag(k_t, hm)], axis=0), NT)
             for (_, _, a_t, r_t, b_t, k_t, _, _, _, _) in units]
        a_ab = [jnp.where(strict, x[0:CHUNK, 0:LANES], 0.0) for x in m]
        a_ak = [jnp.where(strict, x[0:CHUNK, LANES:2 * LANES], 0.0) for x in m]
        a_rk = [jnp.where(incl, x[CHUNK:2 * CHUNK, LANES:2 * LANES], 0.0) for x in m]
        for x, un in zip(m, units):
            arb_s[un[0], un[1]] = jnp.where(incl, x[CHUNK:2 * CHUNK, 0:LANES], 0.0)
        inv = [eye + x for x in a_ab]
        pw = [_dot(x, _block_diag(x, hm)) for x in a_ab]
        for it in range(n_double - 1):
            both = [_dot(jnp.concatenate([p_, i_], axis=0), _block_diag(p_, hm)) for p_, i_ in zip(pw, inv)]
            pw = [x[0:CHUNK] for x in both]
            inv = [i_ + x[CHUNK:2 * CHUNK] for i_, x in zip(inv, both)]
        inv = [i_ + _dot(i_, _block_diag(p_, hm)) for i_, p_ in zip(inv, pw)]
        t1ov = [_dot(jnp.concatenate([x, y], axis=0), _block_diag(un[6], hm))
                for x, y, un in zip(a_ak, a_rk, units)]
        wtuv = [_dot(i_, jnp.concatenate([_block_diag(un[2], hm), _block_diag(x[0:CHUNK], hm)], axis=1))
                for i_, un, x in zip(inv, units, t1ov)]
        wt = [x[:, 0:LANES] for x in wtuv]
        uv = [x[:, LANES:2 * LANES] for x in wtuv]
        pm = [_dot(w_, un[7], TN) for w_, un in zip(wt, units)]
        cm = [_dot(jnp.concatenate([x, un[6]], axis=0), jnp.concatenate([un[7], un[8]], axis=0), TN)
              for x, un in zip(uv, units)]
        for i, un in enumerate(units):
            wt_s[un[0], un[1]] = wt[i]
            uv_s[un[0], un[1]] = uv[i]
            ov_s[un[0], un[1]] = t1ov[i][CHUNK:2 * CHUNK]
            pm_s[un[9]] = jnp.where(same_head, pm[i], 0.0)
            cm_s[un[9]] = jnp.where(same_head, cm[i], 0.0)

    n_chunks = rb // CHUNK
    assert n_chunks % R_PRE == 0
    for cc in range(n_chunks // R_PRE):
        precompute(cc)

    inv_n = 1.0 / R_HEAD_DIM

    def finish(rows, o):
        mu = head_sum(o) * inv_n
        d = o - mu
        var = head_sum(d * d) * inv_n
        o = d * lax.rsqrt(var + R_GN_EPS) * gnw_ref[...] + gnb_ref[...]
        o = o + head_sum(r_s[rows, :] * k_s[rows, :] * rk_ref[...]) * v_s[rows, :]
        o_ref[rows, :] = (o * g_s[rows, :]).astype(o_ref.dtype)

    sts = [st_ref[pr] for pr in range(N_PAIRS)]
    pending = None
    for c in range(n_chunks):
        rows = slice(c * CHUNK, (c + 1) * CHUNK)
        dec = dec_s[c:c + 1, :]
        new_sts = [st * dec[:, ps] + _dot(st, pm_s[c * N_PAIRS + pr]) + cm_s[c * N_PAIRS + pr]
                   for pr, (ps, st) in enumerate(zip(pairs, sts))]
        uo = [_dot(jnp.concatenate([wt_s[rows, ps], rt_s[rows, ps]], axis=0), st, NT)
              for ps, st in zip(pairs, sts)]
        u = [x[0:CHUNK] + uv_s[rows, ps] for x, ps in zip(uo, pairs)]
        o1 = [x[CHUNK:2 * CHUNK] for x in uo]
        o2 = [_dot(arb_s[rows, ps], _block_diag(u_, hm)) for ps, u_ in zip(pairs, u)]
        sts = new_sts
        if pending is not None:
            finish(*pending)
        pending = (rows, jnp.concatenate(
            [o1[pr] + o2[pr] + ov_s[rows, ps] for pr, ps in enumerate(pairs)], axis=1))
    finish(*pending)
    for pr in range(N_PAIRS):
        st_ref[pr] = sts[pr]


def _rwkv(p, v_first, prm, first, batch, lp):
    rb = SEQ_BLOCK
    nb = lp // rb
    t = p.shape[0]

    def col(off, width):
        return pl.BlockSpec((rb, width), lambda b, j: (b * nb + j, off // width))

    def full(a):
        return pl.BlockSpec(a.shape, lambda b, j: (0,) * a.ndim)

    row_spec = pl.BlockSpec((rb, R_WIDTH), lambda b, j: (b * nb + j, 0))
    acts = [p, p, p]
    act_specs = [col(C_RKV, R_SHIFT_W), col(C_G1, R_GATE_RANK), col(C_WA, 2 * LANES)]
    mus = [prm["mu_rkv"], prm["mu_g1"], prm["mu_wa"]]
    tail = []
    if not first:
        acts += [p, v_first]
        act_specs += [col(C_VRES, LANES), row_spec]
        mus.append(prm["mu_vr"])
        tail = [prm["v0"], prm["v2"]]
    consts = mus + [prm[n] for n in ("w0", "w2", "a0", "a2", "g2", "k_k", "k_a", "r_k", "gn_w", "gn_b")] + tail
    out_shape = [jax.ShapeDtypeStruct((t, R_WIDTH), BF16)]
    out_specs = [row_spec]
    if first:
        out_shape.append(jax.ShapeDtypeStruct((t, R_WIDTH), F32))
        out_specs.append(row_spec)
    ext_w = [R_SHIFT_W, R_GATE_RANK, 2 * LANES] + ([] if first else [LANES])
    scratch = ([pltpu.VMEM((N_PAIRS, LANES, LANES), F32)]
               + [pltpu.VMEM((rb + HIST, w), F32) for w in ext_w]
               + [pltpu.VMEM((max(rb // CHUNK, 8), R_WIDTH), F32)]
               + [pltpu.VMEM((rb, R_WIDTH), F32) for _ in range(12)]
               + [pltpu.VMEM((rb // CHUNK * N_PAIRS, LANES, LANES), F32) for _ in range(2)])
    return pl.pallas_call(
        functools.partial(_rwkv_kernel, first=first, rb=rb),
        grid=(batch, nb),
        in_specs=act_specs + [full(a) for a in consts],
        out_specs=out_specs,
        out_shape=out_shape,
        scratch_shapes=scratch,
        compiler_params=_params(("parallel", "arbitrary")),
        name="rwkv7",
    )(*acts, *consts)


def _merge_kernel(oh_ref, om_ref, or_ref, g0_ref, g1_ref, g2_ref, wh_ref, wm_ref, wr_ref, o_ref):
    def gated(g_ref, x_ref, w_ref):
        return _sigmoid(g_ref[...].astype(F32)) * jnp.dot(x_ref[...], w_ref[...], preferred_element_type=F32)

    acc = gated(g0_ref, oh_ref, wh_ref) + gated(g1_ref, om_ref, wm_ref) + gated(g2_ref, or_ref, wr_ref)
    o_ref[...] = acc.astype(o_ref.dtype)


def _merge(o_h, o_m, o_r, p, w_h, w_m, w_r, tm, tn):
    t = p.shape[0]
    gb = C_GATE // tn
    nt = D_MODEL // tn

    def act():
        return pl.BlockSpec((tm, H_WIDTH), lambda i, j: (i, 0))

    def gate(k):
        return pl.BlockSpec((tm, tn), lambda i, j, k=k: (i, gb + k * nt + j))

    def wt():
        return pl.BlockSpec((H_WIDTH, tn), lambda i, j: (0, j))

    return pl.pallas_call(
        _merge_kernel,
        grid=(t // tm, nt),
        in_specs=[act(), act(), act(), gate(0), gate(1), gate(2), wt(), wt(), wt()],
        out_specs=pl.BlockSpec((tm, tn), lambda i, j: (i, j)),
        out_shape=jax.ShapeDtypeStruct((t, D_MODEL), BF16),
        compiler_params=_params(("parallel", "arbitrary")),
        name="merge",
    )(o_h, o_m, o_r, p, p, p, w_h, w_m, w_r)


def _row_valid(tm, lp):
    pos = (pl.program_id(0) * tm + _iota((tm, 1), 0)) % lp
    return (pos >= N_PAD).astype(F32)


def _tail_kernel(h_ref, m_ref, wo_ref, lnw_ref, w1_ref, w2_ref, lnf_ref, o_ref, u_ref, *, tm, lp, final):
    kk = pl.program_id(1)

    @pl.when(kk == 0)
    def _():
        valid = _row_valid(tm, lp)
        mixed = jnp.where(valid > 0.0, m_ref[...], jnp.zeros((), m_ref.dtype))
        x = h_ref[...] + jnp.dot(mixed, wo_ref[...], preferred_element_type=F32)
        ms = jnp.mean(x * x, axis=-1, keepdims=True)
        u_ref[...] = (x * lax.rsqrt(ms + NORM_EPS) * lnw_ref[...] * valid).astype(BF16)
        o_ref[...] = x

    a = jnp.maximum(jnp.dot(u_ref[...], w1_ref[...], preferred_element_type=F32), 0.0)
    o_ref[...] += jnp.dot((a * a).astype(BF16), w2_ref[...], preferred_element_type=F32)

    if final:
        @pl.when(kk == pl.num_programs(1) - 1)
        def _():
            y = o_ref[...]
            ms = jnp.mean(y * y, axis=-1, keepdims=True)
            o_ref[...] = y * lax.rsqrt(ms + NORM_EPS) * lnf_ref[...]


def _tail(h, mixed, w_out, lnw, w1, w2, lnf, tm, th, lp, final):
    t = h.shape[0]
    return pl.pallas_call(
        functools.partial(_tail_kernel, tm=tm, lp=lp, final=final),
        grid=(t // tm, MLP_HIDDEN // th),
        in_specs=[pl.BlockSpec((tm, D_MODEL), lambda i, k: (i, 0)),
                  pl.BlockSpec((tm, D_MODEL), lambda i, k: (i, 0)),
                  pl.BlockSpec((D_MODEL, D_MODEL), lambda i, k: (0, 0), pipeline_mode=pl.Buffered(1)),
                  pl.BlockSpec((1, D_MODEL), lambda i, k: (0, 0)),
                  pl.BlockSpec((D_MODEL, th), lambda i, k: (0, k)),
                  pl.BlockSpec((th, D_MODEL), lambda i, k: (k, 0)),
                  pl.BlockSpec((1, D_MODEL), lambda i, k: (0, 0))],
        out_specs=pl.BlockSpec((tm, D_MODEL), lambda i, k: (i, 0)),
        out_shape=jax.ShapeDtypeStruct((t, D_MODEL), F32),
        scratch_shapes=[pltpu.VMEM((tm, D_MODEL), BF16)],
        compiler_params=_params(("parallel", "arbitrary")),
        name="out_mlp",
    )(h, mixed, w_out, lnw, w1, w2, lnf)


def _pad_cols(a, width):
    return jnp.pad(a, ((0, 0), (0, width - a.shape[1])))


def _pad_rows(a, height):
    return jnp.pad(a, ((0, height - a.shape[0]), (0, 0)))


def _split_in_cols(a, a_vres):
    o = 0
    hcols = a[:, o:o + H_COLS]; o += H_COLS
    z = a[:, o:o + M_WIDTH]; o += M_WIDTH
    xm = a[:, o:o + M_WIDTH]; o += M_WIDTH
    bc = a[:, o:o + 2 * M_BC]; o += 2 * M_BC
    dt = a[:, o:o + M_HEADS]; o += M_HEADS
    rkv = a[:, o:o + 3 * R_WIDTH]; o += 3 * R_WIDTH
    w1 = a[:, o:o + R_DECAY_RANK]; o += R_DECAY_RANK
    a1 = a[:, o:o + R_AAA_RANK]; o += R_AAA_RANK
    g1 = a[:, o:o + R_GATE_RANK]; o += R_GATE_RANK
    gates = a[:, o:o + GATE_COLS]; o += GATE_COLS
    assert o == IN_COLS
    out = jnp.concatenate([hcols, gates, z, xm, rkv, bc, g1, _pad_cols(w1, LANES), _pad_cols(a1, LANES),
                           _pad_cols(dt, LANES), _pad_cols(a_vres, LANES)], axis=1)
    assert out.shape[1] == P_COLS
    return out


def _tile(t, candidates):
    for c in candidates:
        if t % c == 0:
            return c
    raise ValueError(f"no tile for {t}")


def kernel(x, meta, ln1_w, ln2_w, lnf_w, w_in, w_in_vres, hg_lb_logits, hg_norm_w, m_conv_w, m_conv_b, m_dt_bias, m_a_log, m_d, m_norm_w, r_mu, r_mu_vres, r_w0, r_w2, r_a0, r_a2, r_v0, r_v2, r_g2, r_k_k, r_k_a, r_r_k, r_gn_w, r_gn_b, w_up_h, w_up_m, w_up_r, w_out, w_mlp_in, w_mlp_out):
    batch, seq, d = x.shape
    assert d == D_MODEL and M_HEAD_DIM == CHUNK and R_HEAD_DIM == CHUNK
    l_real = seq + CHUNK
    lp = -(-l_real // ROW_ALIGN) * ROW_ALIGN
    t = batch * lp
    tm_proj = _tile(t, (1280, 640))
    tm = _tile(t, (640,))
    tm_tail = _tile(t, (512, 256))

    h = jnp.concatenate([jnp.zeros((batch, N_PAD, d), x.dtype),
                         jnp.broadcast_to(meta.astype(x.dtype), (batch, N_META, d)),
                         x, jnp.zeros((batch, lp - l_real, d), x.dtype)], axis=1).reshape(t, d)

    v_first = None
    for l in range(DEPTH):
        vres_w = w_in_vres[l - 1] if l > 0 else jnp.zeros((d, R_MV_RANK), w_in.dtype)
        w_comb = _split_in_cols(w_in[l], vres_w).astype(BF16)
        p = _proj(h, ln1_w[l][None, :], w_comb, tm_proj, 1280)

        o_h = _hgrn(p, hg_lb_logits, hg_norm_w[l].reshape(1, H_WIDTH), l, batch, lp)
        o_m = _mamba(p, m_conv_w[l], m_conv_b[l], m_dt_bias[l], m_a_log[l], m_d[l], m_norm_w[l], batch, lp)

        mu = r_mu[l][None, :]
        o = 3 * R_WIDTH
        prm = {
            "mu_rkv": mu[:, :o],
            "mu_wa": jnp.concatenate([_pad_cols(mu[:, o:o + R_DECAY_RANK], LANES),
                                      _pad_cols(mu[:, o + R_DECAY_RANK:o + R_DECAY_RANK + R_AAA_RANK], LANES)], axis=1),
            "mu_g1": mu[:, o + R_DECAY_RANK + R_AAA_RANK:],
            "w0": r_w0[l][None, :], "w2": _pad_rows(r_w2[l], LANES).astype(BF16),
            "a0": r_a0[l][None, :], "a2": _pad_rows(r_a2[l], LANES).astype(BF16),
            "g2": r_g2[l].astype(BF16),
            "k_k": r_k_k[l][None, :], "k_a": r_k_a[l][None, :], "r_k": r_r_k[l].reshape(1, R_WIDTH),
            "gn_w": r_gn_w[l][None, :], "gn_b": r_gn_b[l][None, :],
        }
        if l == 0:
            o_r, v_first = _rwkv(p, None, prm, True, batch, lp)
        else:
            prm["mu_vr"] = _pad_cols(r_mu_vres[l - 1][None, :], LANES)
            prm["v0"] = r_v0[l - 1][None, :]
            prm["v2"] = _pad_rows(r_v2[l - 1], LANES).astype(BF16)
            (o_r,) = _rwkv(p, v_first, prm, False, batch, lp)

        mixed = _merge(o_h, o_m, o_r, p, w_up_h[l].astype(BF16), w_up_m[l].astype(BF16),
                       w_up_r[l].astype(BF16), tm, 1024)
        h = _tail(h, mixed, w_out[l].astype(BF16), ln2_w[l][None, :], w_mlp_in[l].astype(BF16),
                  w_mlp_out[l].astype(BF16), lnf_w[None, :], tm_tail, 1024, lp, l == DEPTH - 1)

    return h.reshape(batch, lp, d)[:, CHUNK:l_real]
```

```python
import functools

import jax
import jax.numpy as jnp
from jax import lax
from jax.experimental import pallas as pl
from jax.experimental.pallas import tpu as pltpu

F32 = jnp.float32
BF16 = jnp.bfloat16

LANES = 128

D_MODEL = 2048
DEPTH = 2
N_META = 16
CHUNK = 64
N_PAD = CHUNK - N_META
MLP_HIDDEN = 4 * D_MODEL
NORM_EPS = 1e-6
N_BRANCH = 3
L2_EPS = 1e-24

H_WIDTH = D_MODEL // 2
H_HEAD_DIM = 128
H_HEADS = H_WIDTH // H_HEAD_DIM

M_WIDTH = D_MODEL // 2
M_HEAD_DIM = 64
M_HEADS = M_WIDTH // M_HEAD_DIM
M_GROUPS = 2
M_STATE = 128
M_CONV = 4
M_BC = M_GROUPS * M_STATE

R_WIDTH = D_MODEL // 2
R_HEAD_DIM = 64
R_HEADS = R_WIDTH // R_HEAD_DIM
R_DECAY_RANK = max(32, int(round(1.8 * D_MODEL ** 0.5 / 32)) * 32)
R_AAA_RANK = max(32, int(round(1.8 * D_MODEL ** 0.5 / 32)) * 32)
R_MV_RANK = max(32, int(round(1.3 * D_MODEL ** 0.5 / 32)) * 32)
R_GATE_RANK = max(32, int(round(0.6 * D_MODEL ** 0.8 / 32)) * 32)
R_GN_EPS = 64e-5
N_PAIRS = R_HEADS // 2

H_COLS = 4 * H_WIDTH
M_CONV_CH = M_WIDTH + 2 * M_BC
M_COLS = M_WIDTH + M_CONV_CH + M_HEADS
R_COLS = 3 * R_WIDTH + R_DECAY_RANK + R_AAA_RANK + R_GATE_RANK
GATE_COLS = N_BRANCH * D_MODEL
IN_COLS = H_COLS + M_COLS + R_COLS + GATE_COLS

C_H = 0
C_GATE = C_H + H_COLS
C_Z = C_GATE + GATE_COLS
C_X = C_Z + M_WIDTH
C_RKV = C_X + M_WIDTH
C_BC = C_RKV + 3 * R_WIDTH
C_G1 = C_BC + 2 * M_BC
C_WA = C_G1 + R_GATE_RANK
C_DT = C_WA + 2 * LANES
C_VRES = C_DT + LANES
P_COLS = C_VRES + LANES

ROW_ALIGN = 1280
SEQ_BLOCK = 256
HG_BLOCK = 640
VMEM_LIMIT = 56 * 1024 * 1024

NN = (((1,), (0,)), ((), ()))
NT = (((1,), (1,)), ((), ()))
TN = (((0,), (0,)), ((), ()))


def _dot(a, b, dims=NN):
    return lax.dot_general(a.astype(BF16), b.astype(BF16), dims, preferred_element_type=F32)


def _split(x, terms):
    out = []
    for _ in range(terms - 1):
        hi = x.astype(BF16)
        out.append(hi)
        x = x - hi.astype(F32)
    out.append(x.astype(BF16))
    return out


def _mask_dot_rhs(m, x, terms=3):
    mb = m.astype(BF16)
    return jnp.dot(jnp.concatenate([mb] * terms, axis=1), jnp.concatenate(_split(x, terms), axis=0),
                   preferred_element_type=F32)


def _mask_dot_lhs(x, m, terms=3):
    mb = m.astype(BF16)
    return jnp.dot(jnp.concatenate(_split(x, terms), axis=1), jnp.concatenate([mb] * terms, axis=0),
                   preferred_element_type=F32)


LOG2E = 1.4426950408889634


def _sigmoid(x):
    return 1.0 / (1.0 + jnp.exp2(x * -LOG2E))


def _silu(x):
    return x * _sigmoid(x)


def _softplus(x):
    return jnp.maximum(x, 0.0) + jnp.log(1.0 + jnp.exp(-jnp.abs(x)))


def _iota(shape, dim):
    return lax.broadcasted_iota(jnp.int32, shape, dim)


def _tri(n):
    return (_iota((n, n), 0) >= _iota((n, n), 1)).astype(F32)


def _head_masks(width=LANES):
    lane = _iota((CHUNK, width), 1)
    return [lane // CHUNK == hd for hd in range(width // CHUNK)]


def _block_diag(x, masks):
    return jnp.concatenate([jnp.where(m_, x, 0.0) for m_ in masks], axis=0)


def _params(sem):
    return pltpu.CompilerParams(dimension_semantics=sem, vmem_limit_bytes=VMEM_LIMIT)


def _proj_kernel(h_ref, lnw_ref, w_ref, o_ref, u_ref):
    @pl.when(pl.program_id(1) == 0)
    def _():
        x = h_ref[...]
        ms = jnp.mean(x * x, axis=-1, keepdims=True)
        u_ref[...] = (x * lax.rsqrt(ms + NORM_EPS) * lnw_ref[...]).astype(BF16)

    o_ref[...] = jnp.dot(u_ref[...], w_ref[...], preferred_element_type=F32).astype(o_ref.dtype)


def _proj(h, lnw, w, tm, tn):
    t = h.shape[0]
    return pl.pallas_call(
        _proj_kernel,
        grid=(t // tm, P_COLS // tn),
        in_specs=[
            pl.BlockSpec((tm, D_MODEL), lambda i, j: (i, 0)),
            pl.BlockSpec((1, D_MODEL), lambda i, j: (0, 0)),
            pl.BlockSpec((D_MODEL, tn), lambda i, j: (0, j)),
        ],
        out_specs=pl.BlockSpec((tm, tn), lambda i, j: (i, j)),
        out_shape=jax.ShapeDtypeStruct((t, P_COLS), BF16),
        scratch_shapes=[pltpu.VMEM((tm, D_MODEL), BF16)],
        compiler_params=_params(("parallel", "arbitrary")),
        name="in_proj",
    )(h, lnw, w)


HG_SUB = 8
HG_UNROLL = 2


def _hgrn_kernel(q_ref, f_ref, i_ref, g_ref, lbl_ref, nw_ref, o_ref,
                 st_ref, *, layer, rb):
    jb = pl.program_id(1)

    @pl.when(jb == 0)
    def _():
        st_ref[...] = jnp.zeros(st_ref.shape, F32)

    lg = lbl_ref[...]
    mx = jnp.max(lg, axis=0, keepdims=True)
    ex = jnp.exp(lg - mx)
    sm = ex / jnp.sum(ex, axis=0, keepdims=True)
    lb = jnp.sum(sm[0:layer + 1], axis=0, keepdims=True) - sm[0:1]

    assert HG_SUB == 8
    col8 = _iota((HG_SUB, H_HEAD_DIM), 1)
    key_row = [(_iota((HG_SUB, H_HEAD_DIM), 0) + (HG_SUB - d)) % HG_SUB for d in range(HG_SUB)]
    heads = [slice(hd * H_HEAD_DIM, (hd + 1) * H_HEAD_DIM) for hd in range(H_HEADS)]
    arow = _iota((CHUNK, CHUNK), 0)
    acol = _iota((CHUNK, CHUNK), 1)
    off_mask = acol < (arow // HG_SUB) * HG_SUB
    causal = arow >= acol
    n_sub = CHUNK // HG_SUB

    tri = _tri(CHUNK)
    row = _iota((CHUNK, 1), 0)

    def one_chunk(r0):
        rows = pl.ds(r0, CHUNK)
        valid = ((jb * rb + r0 + row) >= N_PAD).astype(F32)
        q = _silu(q_ref[rows, :].astype(F32))
        fr = f_ref[rows, :].astype(F32)
        sig = _sigmoid(fr)
        k = (1.0 - lb) * (1.0 - sig) * valid
        v = i_ref[rows, :].astype(F32)
        g2 = _mask_dot_rhs(tri, jnp.log(lb + (1.0 - lb) * sig)) * LOG2E
        hk = g2 - jnp.log(k) * LOG2E

        diag = [[None] * n_sub for _ in heads]
        for sub in range(n_sub):
            base = sub * HG_SUB
            sub_rows = slice(base, base + HG_SUB)
            g_i, q_i, k_i, h_i = g2[sub_rows], q[sub_rows], k[sub_rows], hk[sub_rows]
            acc = [jnp.zeros((HG_SUB, H_HEAD_DIM), F32) for _ in heads]
            for d in range(HG_SUB):
                if d == 0:
                    e = q_i * k_i
                else:
                    e = jnp.exp2(jnp.minimum(g_i - pltpu.roll(h_i, d, axis=0), 0.0)) * q_i
                at_key = col8 == key_row[d] + base
                for hd, hs in enumerate(heads):
                    s = jnp.sum(e[:, hs], axis=-1, keepdims=True)
                    acc[hd] = jnp.where(at_key, s, acc[hd])
            for hd in range(H_HEADS):
                diag[hd][sub] = acc[hd]

        g_last = g2[CHUNK - 1:CHUNK, :]
        qg = q * jnp.exp2(g2)
        kdec = jnp.exp2(g_last - hk)
        e_last = jnp.exp2(g_last)
        attn = []
        for hd, hs in enumerate(heads):
            g_h, q_h, hk_h = g2[:, hs], q[:, hs], hk[:, hs]
            qcat, kcat = [], []
            for sub in range(1, n_sub):
                base = sub * HG_SUB
                g_r = g_h[base:base + 1, :]
                q_sub = q_h[base:base + HG_SUB] * jnp.exp2(jnp.minimum(g_h[base:base + HG_SUB] - g_r, 0.0))
                pieces = [jnp.zeros((base, H_HEAD_DIM), F32), q_sub]
                if base + HG_SUB < CHUNK:
                    pieces.append(jnp.zeros((CHUNK - base - HG_SUB, H_HEAD_DIM), F32))
                qcat.append(jnp.concatenate(pieces, axis=0))
                k_sub = jnp.exp2(jnp.minimum(g_r - hk_h[0:base], 0.0))
                kcat.append(jnp.concatenate([k_sub, jnp.zeros((CHUNK - base, H_HEAD_DIM), F32)], axis=0))
            a_off = _dot(jnp.concatenate(qcat, axis=1), jnp.concatenate(kcat, axis=1), NT)
            a_diag = jnp.concatenate(diag[hd], axis=0)[:, 0:CHUNK]
            attn.append(jnp.where(off_mask, a_off, jnp.where(causal, a_diag, 0.0)))
        sts = [st_ref[hd] for hd in range(H_HEADS)]
        inter = [_dot(qg[:, hs], sts[hd], NT) for hd, hs in enumerate(heads)]
        upd = [_dot(v[:, hs], kdec[:, hs], TN) for hs in heads]
        intra = [_dot(attn[hd], v[:, hs]) for hd, hs in enumerate(heads)]
        outs = []
        for hd, hs in enumerate(heads):
            st_ref[hd] = sts[hd] * e_last[:, hs] + upd[hd]
            o_h = (intra[hd] + inter[hd]) * _sigmoid(g_ref[rows, hs].astype(F32))
            outs.append(o_h * lax.rsqrt(jnp.mean(o_h * o_h, axis=-1, keepdims=True) + NORM_EPS))
        o_ref[rows, :] = (jnp.concatenate(outs, axis=-1) * nw_ref[...]).astype(o_ref.dtype)

    def chunks(cc, carry):
        for u in range(HG_UNROLL):
            one_chunk(pl.multiple_of((cc * HG_UNROLL + u) * CHUNK, CHUNK))
        return carry

    assert (rb // CHUNK) % HG_UNROLL == 0
    lax.fori_loop(0, rb // CHUNK // HG_UNROLL, chunks, 0)


def _hgrn(p, lb_logits, norm_w, layer, batch, lp):
    rb = HG_BLOCK
    nb = lp // rb
    t = p.shape[0]
    cb = C_H // H_WIDTH

    def col(k):
        return pl.BlockSpec((rb, H_WIDTH), lambda b, j, k=k: (b * nb + j, cb + k))

    return pl.pallas_call(
        functools.partial(_hgrn_kernel, layer=layer, rb=rb),
        grid=(batch, nb),
        in_specs=[col(0), col(1), col(2), col(3),
                  pl.BlockSpec((DEPTH, H_WIDTH), lambda b, j: (0, 0)),
                  pl.BlockSpec((1, H_WIDTH), lambda b, j: (0, 0))],
        out_specs=pl.BlockSpec((rb, H_WIDTH), lambda b, j: (b * nb + j, 0)),
        out_shape=jax.ShapeDtypeStruct((t, H_WIDTH), BF16),
        scratch_shapes=[pltpu.VMEM((H_HEADS, H_HEAD_DIM, H_HEAD_DIM), F32)],
        compiler_params=_params(("parallel", "arbitrary")),
        name="hgrn2",
    )(p, p, p, p, lb_logits, norm_w)


HIST = 8


def _mamba_kernel(z_ref, x_ref, bc_ref, dt_ref, cwx_ref, cwb_ref, cbx_ref, cbb_ref,
                  dtb_ref, alog_ref, dsk_ref, nw_ref, expand_ref, tri_ref, ones_ref, le_ref, ge_ref, o_ref,
                  st_ref, xe, be, xa, ba, xc_s, da_s, *, rb):
    jb = pl.program_id(1)

    @pl.when(jb == 0)
    def _():
        st_ref[...] = jnp.zeros(st_ref.shape, F32)
        xe[0:HIST, :] = jnp.zeros((HIST, M_WIDTH), F32)
        be[0:HIST, :] = jnp.zeros((HIST, 2 * M_BC), F32)

    def conv(ext, src_ref, w_ref, b_ref, dst):
        ext[HIST:HIST + rb, :] = src_ref[...].astype(F32)
        full = ext[...]
        acc = b_ref[...] + w_ref[M_CONV - 1:M_CONV, :] * full[HIST:HIST + rb]
        for d in range(1, M_CONV):
            acc = acc + w_ref[M_CONV - 1 - d:M_CONV - d, :] * pltpu.roll(full, d, axis=0)[HIST:HIST + rb]
        dst[...] = _silu(acc)
        ext[0:HIST, :] = ext[rb:rb + HIST, :]

    conv(xe, x_ref, cwx_ref, cbx_ref, xa)
    conv(be, bc_ref, cwb_ref, cbb_ref, ba)

    expand = expand_ref[...]
    valid = ((jb * rb + _iota((rb, 1), 0)) >= N_PAD).astype(F32)
    dt = _softplus(dt_ref[...].astype(F32) + dtb_ref[...]) * valid
    da_s[...] = _mask_dot_lhs(dt * -jnp.exp(alog_ref[...]), expand)
    xc_s[...] = xa[...] * _mask_dot_lhs(dt, expand, terms=2)
    hpg = M_HEADS // M_GROUPS
    hm = _head_masks()

    sts = [st_ref[pr] for pr in range(M_HEADS // 2)]
    for c in range(rb // CHUNK):
        rows = slice(c * CHUNK, (c + 1) * CHUNK)
        xs = xa[rows, :]
        bcm = ba[rows, :]
        da_e = da_s[rows, :]
        acum = _mask_dot_rhs(tri_ref[...], da_e)
        acum_j = _mask_dot_rhs(ones_ref[...], da_e * le_ref[...])
        lmat = jnp.exp(jnp.minimum(acum - acum_j, 0.0)) * ge_ref[...]
        a_last = acum[CHUNK - 1:CHUNK, :]
        e_cum = jnp.exp(acum)
        e_end = jnp.exp(a_last - acum)
        e_last = jnp.exp(a_last)
        xc = xc_s[rows, :]
        scores = []
        for g in range(M_GROUPS):
            b_g = bcm[:, g * M_STATE:(g + 1) * M_STATE]
            c_g = bcm[:, M_BC + g * M_STATE:M_BC + (g + 1) * M_STATE]
            scores.append(_dot(c_g, jnp.concatenate([b_g] * hpg, axis=0), NT))
        attn = jnp.concatenate(scores, axis=1) * lmat
        ys = []
        for pr in range(M_HEADS // 2):
            ps = slice(pr * LANES, (pr + 1) * LANES)
            g = (2 * pr) // hpg
            b_g = bcm[:, g * M_STATE:(g + 1) * M_STATE]
            c_g = bcm[:, M_BC + g * M_STATE:M_BC + (g + 1) * M_STATE]
            xc_p = xc[:, ps]
            y = _dot(attn[:, ps], _block_diag(xc_p, hm)) + _dot(c_g, sts[pr]) * e_cum[:, ps]
            sts[pr] = sts[pr] * e_last[:, ps] + _dot(b_g, xc_p * e_end[:, ps], TN)
            ys.append(y)
        y = (jnp.concatenate(ys, axis=1) + dsk_ref[...] * xs) * _silu(z_ref[rows, :].astype(F32))
        gw = M_WIDTH // M_GROUPS
        outs = []
        for g in range(M_GROUPS):
            yg = y[:, g * gw:(g + 1) * gw]
            outs.append(yg * lax.rsqrt(jnp.mean(yg * yg, axis=-1, keepdims=True) + NORM_EPS))
        o_ref[rows, :] = (jnp.concatenate(outs, axis=1) * nw_ref[...]).astype(o_ref.dtype)
    for pr in range(M_HEADS // 2):
        st_ref[pr] = sts[pr]


def _mamba(p, conv_w, conv_b, dt_bias, a_log, d_skip, norm_w, batch, lp):
    rb = SEQ_BLOCK
    nb = lp // rb
    t = p.shape[0]

    def col(off, width):
        return pl.BlockSpec((rb, width), lambda b, j: (b * nb + j, off // width))

    def full(shape):
        return pl.BlockSpec(shape, lambda b, j: (0,) * len(shape))

    pad = LANES - M_HEADS
    dtb = jnp.pad(dt_bias.reshape(1, M_HEADS), ((0, 0), (0, pad)))
    alog = jnp.pad(a_log.reshape(1, M_HEADS), ((0, 0), (0, pad)))
    dsk = jnp.repeat(d_skip, M_HEAD_DIM).reshape(1, M_WIDTH)
    expand = (_iota((LANES, M_WIDTH), 1) // M_HEAD_DIM == _iota((LANES, M_WIDTH), 0)).astype(BF16)
    row_i = _iota((CHUNK, M_WIDTH), 0)
    pos_j = _iota((CHUNK, M_WIDTH), 1) % CHUNK
    consts = [expand, _tri(CHUNK).astype(BF16), jnp.ones((CHUNK, CHUNK), BF16),
              (row_i <= pos_j).astype(F32), (row_i >= pos_j).astype(F32)]
    return pl.pallas_call(
        functools.partial(_mamba_kernel, rb=rb),
        grid=(batch, nb),
        in_specs=[col(C_Z, M_WIDTH), col(C_X, M_WIDTH), col(C_BC, 2 * M_BC), col(C_DT, LANES),
                  full((M_CONV, M_WIDTH)), full((M_CONV, 2 * M_BC)),
                  full((1, M_WIDTH)), full((1, 2 * M_BC)),
                  full((1, LANES)), full((1, LANES)), full((1, M_WIDTH)), full((1, M_WIDTH))]
        + [full(c.shape) for c in consts],
        out_specs=pl.BlockSpec((rb, M_WIDTH), lambda b, j: (b * nb + j, 0)),
        out_shape=jax.ShapeDtypeStruct((t, M_WIDTH), BF16),
        scratch_shapes=[pltpu.VMEM((M_HEADS // 2, M_STATE, LANES), F32),
                        pltpu.VMEM((rb + HIST, M_WIDTH), F32),
                        pltpu.VMEM((rb + HIST, 2 * M_BC), F32),
                        pltpu.VMEM((rb, M_WIDTH), F32),
                        pltpu.VMEM((rb, 2 * M_BC), F32)]
        + [pltpu.VMEM((rb, M_WIDTH), F32) for _ in range(2)],
        compiler_params=_params(("parallel", "arbitrary")),
        name="mamba2",
    )(p, p, p, p, conv_w[:, :M_WIDTH], conv_w[:, M_WIDTH:], conv_b[None, :M_WIDTH],
      conv_b[None, M_WIDTH:], dtb, alog, dsk, norm_w[None, :], *consts)


R_SHIFT_W = 3 * R_WIDTH
R_PRE = 2
R_SLAB = LANES


def _rwkv_kernel(*refs, first, rb):
    if first:
        (rkv_ref, g1_ref, wa_ref, mu_rkv, mu_g1, mu_wa,
         w0_ref, w2_ref, a0_ref, a2_ref, g2_ref, kk_ref, ka_ref, rk_ref, gnw_ref, gnb_ref,
         o_ref, vf_out,
         st_ref, e_rkv, e_g1, e_wa, dec_s,
         lw_s, r_s, k_s, v_s, a_s, b_s, g_s, rt_s, wt_s, uv_s, arb_s, ov_s, pm_s, cm_s) = refs
    else:
        (rkv_ref, g1_ref, wa_ref, vr_ref, vf_ref, mu_rkv, mu_g1, mu_wa, mu_vr,
         w0_ref, w2_ref, a0_ref, a2_ref, g2_ref, kk_ref, ka_ref, rk_ref, gnw_ref, gnb_ref,
         v0_ref, v2_ref,
         o_ref,
         st_ref, e_rkv, e_g1, e_wa, e_vr, dec_s,
         lw_s, r_s, k_s, v_s, a_s, b_s, g_s, rt_s, wt_s, uv_s, arb_s, ov_s, pm_s, cm_s) = refs
    jb = pl.program_id(1)
    exts = [e_rkv, e_g1, e_wa] + ([] if first else [e_vr])

    @pl.when(jb == 0)
    def _():
        st_ref[...] = jnp.zeros(st_ref.shape, F32)
        for ext in exts:
            ext[0:HIST, :] = jnp.zeros((HIST, ext.shape[1]), F32)

    def shift(ext, src_ref, mu_ref):
        ext[HIST:HIST + rb, :] = src_ref[...].astype(F32)
        full = ext[...]
        cur = full[HIST:HIST + rb]
        prev = pltpu.roll(full, 1, axis=0)[HIST:HIST + rb]
        out = cur + (prev - cur) * mu_ref[...]
        ext[0:HIST, :] = ext[rb:rb + HIST, :]
        return out

    rkv = shift(e_rkv, rkv_ref, mu_rkv)
    gl = shift(e_g1, g1_ref, mu_g1)
    wa = shift(e_wa, wa_ref, mu_wa)
    r = rkv[:, 0:R_WIDTH]
    k = rkv[:, R_WIDTH:2 * R_WIDTH]
    v = rkv[:, 2 * R_WIDTH:3 * R_WIDTH]
    wl = wa[:, 0:LANES]
    al = wa[:, LANES:2 * LANES]

    valid = ((jb * rb + _iota((rb, 1), 0)) >= N_PAD).astype(F32)
    w_log = -_softplus(-(w0_ref[...] + _dot(jnp.tanh(wl), w2_ref[...]))) - 0.5
    lw_s[...] = -jnp.exp(w_log)
    a = _sigmoid(a0_ref[...] + _dot(al, a2_ref[...]))
    if first:
        vf_out[...] = v
    else:
        vl = shift(e_vr, vr_ref, mu_vr)
        v = v + (vf_ref[...] - v) * _sigmoid(v0_ref[...] + _dot(vl, v2_ref[...]))
    g_s[...] = _dot(_sigmoid(gl), g2_ref[...])

    seg_ones = (_iota((LANES, LANES), 0) // R_HEAD_DIM == _iota((LANES, LANES), 1) // R_HEAD_DIM).astype(F32)

    def head_sum(x):
        return jnp.concatenate(
            [_mask_dot_lhs(x[:, s * LANES:(s + 1) * LANES], seg_ones, terms=2) for s in range(N_PAIRS)], axis=1)

    kk = k * kk_ref[...]
    kk = kk * lax.rsqrt(jnp.maximum(head_sum(kk * kk), L2_EPS))
    kh = k * (1.0 + (a - 1.0) * ka_ref[...]) * valid
    r_s[...] = r
    k_s[...] = kh
    v_s[...] = v
    a_s[...] = -kk
    b_s[...] = kk * a

    tri = _tri(CHUNK)
    sl = R_SLAB
    t_i = _iota((CHUNK, sl), 0)
    s_j = _iota((CHUNK, sl), 1) % CHUNK
    strict = t_i > s_j
    incl = t_i >= s_j
    eye = (t_i == s_j).astype(F32)
    same_head = _iota((sl, sl), 0) // R_HEAD_DIM == _iota((sl, sl), 1) // R_HEAD_DIM
    n_double = 5
    assert 2 ** (n_double + 1) == CHUNK

    n_slabs = R_WIDTH // sl
    pairs = [slice(i * sl, (i + 1) * sl) for i in range(n_slabs)]
    hm = _head_masks(sl)

    def precompute(cc):
        units = []
        for sub in range(R_PRE):
            c = cc * R_PRE + sub
            rows = slice(c * CHUNK, (c + 1) * CHUNK)
            lw = lw_s[rows, :]
            cum = _mask_dot_rhs(tri, lw)
            c_last = cum[CHUNK - 1:CHUNK, :]
            e_inv = jnp.exp(-cum)
            e_end = jnp.exp(c_last - cum)
            b_c, k_c, v_c = b_s[rows, :], k_s[rows, :], v_s[rows, :]
            a_t = a_s[rows, :] * jnp.exp(cum - lw)
            r_t = r_s[rows, :] * jnp.exp(cum)
            b_t = b_c * e_inv
            k_t = k_c * e_inv
            b_h = b_c * e_end
            k_h = k_c * e_end
            rt_s[rows, :] = r_t
            dec_s[c:c + 1, :] = jnp.exp(c_last)
            units += [(rows, ps, a_t[:, ps], r_t[:, ps], b_t[:, ps], k_t[:, ps], v_c[:, ps],
                       b_h[:, ps], k_h[:, ps], c * n_slabs + pr) for pr, ps in enumerate(pairs)]
        m = [_dot(jnp.concatenate([a_t, r_t], axis=0),
                  jnp.concatenate([_block_diag(b_t, hm), _block_diag(k_t, hm)], axis=0), NT)
             for (_, _, a_t, r_t, b_t, k_t, _, _, _, _) in units]
        a_ab = [jnp.where(strict, x[0:CHUNK, 0:sl], 0.0) for x in m]
        a_ak = [jnp.where(strict, x[0:CHUNK, sl:2 * sl], 0.0) for x in m]
        a_rk = [jnp.where(incl, x[CHUNK:2 * CHUNK, sl:2 * sl], 0.0) for x in m]
        for x, un in zip(m, units):
            arb_s[un[0], un[1]] = jnp.where(incl, x[CHUNK:2 * CHUNK, 0:sl], 0.0)
        inv = [eye + x for x in a_ab]
        pw = [_dot(x, _block_diag(x, hm)) for x in a_ab]
        for it in range(n_double - 1):
            both = [_dot(jnp.concatenate([p_, i_], axis=0), _block_diag(p_, hm)) for p_, i_ in zip(pw, inv)]
            pw = [x[0:CHUNK] for x in both]
            inv = [i_ + x[CHUNK:2 * CHUNK] for i_, x in zip(inv, both)]
        inv = [i_ + _dot(i_, _block_diag(p_, hm)) for i_, p_ in zip(inv, pw)]
        t1ov = [_dot(jnp.concatenate([x, y], axis=0), _block_diag(un[6], hm))
                for x, y, un in zip(a_ak, a_rk, units)]
        wtuv = [_dot(i_, jnp.concatenate([_block_diag(un[2], hm), _block_diag(x[0:CHUNK], hm)], axis=1))
                for i_, un, x in zip(inv, units, t1ov)]
        wt = [x[:, 0:sl] for x in wtuv]
        uv = [x[:, sl:2 * sl] for x in wtuv]
        pm = [_dot(w_, un[7], TN) for w_, un in zip(wt, units)]
        cm = [_dot(jnp.concatenate([x, un[6]], axis=0), jnp.concatenate([un[7], un[8]], axis=0), TN)
              for x, un in zip(uv, units)]
        for i, un in enumerate(units):
            wt_s[un[0], un[1]] = wt[i]
            uv_s[un[0], un[1]] = uv[i]
            ov_s[un[0], un[1]] = t1ov[i][CHUNK:2 * CHUNK]
            pm_s[un[9]] = jnp.where(same_head, pm[i], 0.0)
            cm_s[un[9]] = jnp.where(same_head, cm[i], 0.0)

    n_chunks = rb // CHUNK
    assert n_chunks % R_PRE == 0
    for cc in range(n_chunks // R_PRE):
        precompute(cc)

    inv_n = 1.0 / R_HEAD_DIM

    def finish(rows, o):
        mu = head_sum(o) * inv_n
        d = o - mu
        var = head_sum(d * d) * inv_n
        o = d * lax.rsqrt(var + R_GN_EPS) * gnw_ref[...] + gnb_ref[...]
        o = o + head_sum(r_s[rows, :] * k_s[rows, :] * rk_ref[...]) * v_s[rows, :]
        o_ref[rows, :] = (o * g_s[rows, :]).astype(o_ref.dtype)

    sts = [st_ref[pr] for pr in range(n_slabs)]
    pending = None
    for c in range(n_chunks):
        rows = slice(c * CHUNK, (c + 1) * CHUNK)
        dec = dec_s[c:c + 1, :]
        new_sts = [st * dec[:, ps] + _dot(st, pm_s[c * n_slabs + pr]) + cm_s[c * n_slabs + pr]
                   for pr, (ps, st) in enumerate(zip(pairs, sts))]
        uo = [_dot(jnp.concatenate([wt_s[rows, ps], rt_s[rows, ps]], axis=0), st, NT)
              for ps, st in zip(pairs, sts)]
        u = [x[0:CHUNK] + uv_s[rows, ps] for x, ps in zip(uo, pairs)]
        o1 = [x[CHUNK:2 * CHUNK] for x in uo]
        o2 = [_dot(arb_s[rows, ps], _block_diag(u_, hm)) for ps, u_ in zip(pairs, u)]
        sts = new_sts
        if pending is not None:
            finish(*pending)
        pending = (rows, jnp.concatenate(
            [o1[pr] + o2[pr] + ov_s[rows, ps] for pr, ps in enumerate(pairs)], axis=1))
    finish(*pending)
    for pr in range(n_slabs):
        st_ref[pr] = sts[pr]


def _rwkv(p, v_first, prm, first, batch, lp):
    rb = SEQ_BLOCK
    nb = lp // rb
    t = p.shape[0]

    def col(off, width):
        return pl.BlockSpec((rb, width), lambda b, j: (b * nb + j, off // width))

    def full(a):
        return pl.BlockSpec(a.shape, lambda b, j: (0,) * a.ndim)

    row_spec = pl.BlockSpec((rb, R_WIDTH), lambda b, j: (b * nb + j, 0))
    acts = [p, p, p]
    act_specs = [col(C_RKV, R_SHIFT_W), col(C_G1, R_GATE_RANK), col(C_WA, 2 * LANES)]
    mus = [prm["mu_rkv"], prm["mu_g1"], prm["mu_wa"]]
    tail = []
    if not first:
        acts += [p, v_first]
        act_specs += [col(C_VRES, LANES), row_spec]
        mus.append(prm["mu_vr"])
        tail = [prm["v0"], prm["v2"]]
    consts = mus + [prm[n] for n in ("w0", "w2", "a0", "a2", "g2", "k_k", "k_a", "r_k", "gn_w", "gn_b")] + tail
    out_shape = [jax.ShapeDtypeStruct((t, R_WIDTH), BF16)]
    out_specs = [row_spec]
    if first:
        out_shape.append(jax.ShapeDtypeStruct((t, R_WIDTH), F32))
        out_specs.append(row_spec)
    ext_w = [R_SHIFT_W, R_GATE_RANK, 2 * LANES] + ([] if first else [LANES])
    n_slabs = R_WIDTH // R_SLAB
    scratch = ([pltpu.VMEM((n_slabs, R_SLAB, R_SLAB), F32)]
               + [pltpu.VMEM((rb + HIST, w), F32) for w in ext_w]
               + [pltpu.VMEM((max(rb // CHUNK, 8), R_WIDTH), F32)]
               + [pltpu.VMEM((rb, R_WIDTH), F32) for _ in range(12)]
               + [pltpu.VMEM((rb // CHUNK * n_slabs, R_SLAB, R_SLAB), F32) for _ in range(2)])
    return pl.pallas_call(
        functools.partial(_rwkv_kernel, first=first, rb=rb),
        grid=(batch, nb),
        in_specs=act_specs + [full(a) for a in consts],
        out_specs=out_specs,
        out_shape=out_shape,
        scratch_shapes=scratch,
        compiler_params=_params(("parallel", "arbitrary")),
        name="rwkv7",
    )(*acts, *consts)


def _merge_kernel(oh_ref, om_ref, or_ref, g0_ref, g1_ref, g2_ref, wh_ref, wm_ref, wr_ref, o_ref):
    def gated(g_ref, x_ref, w_ref):
        return _sigmoid(g_ref[...].astype(F32)) * jnp.dot(x_ref[...], w_ref[...], preferred_element_type=F32)

    acc = gated(g0_ref, oh_ref, wh_ref) + gated(g1_ref, om_ref, wm_ref) + gated(g2_ref, or_ref, wr_ref)
    o_ref[...] = acc.astype(o_ref.dtype)


def _merge(o_h, o_m, o_r, p, w_h, w_m, w_r, tm, tn):
    t = p.shape[0]
    gb = C_GATE // tn
    nt = D_MODEL // tn

    def act():
        return pl.BlockSpec((tm, H_WIDTH), lambda i, j: (i, 0))

    def gate(k):
        return pl.BlockSpec((tm, tn), lambda i, j, k=k: (i, gb + k * nt + j))

    def wt():
        return pl.BlockSpec((H_WIDTH, tn), lambda i, j: (0, j))

    return pl.pallas_call(
        _merge_kernel,
        grid=(t // tm, nt),
        in_specs=[act(), act(), act(), gate(0), gate(1), gate(2), wt(), wt(), wt()],
        out_specs=pl.BlockSpec((tm, tn), lambda i, j: (i, j)),
        out_shape=jax.ShapeDtypeStruct((t, D_MODEL), BF16),
        compiler_params=_params(("parallel", "arbitrary")),
        name="merge",
    )(o_h, o_m, o_r, p, p, p, w_h, w_m, w_r)


def _row_valid(tm, lp):
    pos = (pl.program_id(0) * tm + _iota((tm, 1), 0)) % lp
    return (pos >= N_PAD).astype(F32)


def _tail_kernel(h_ref, m_ref, wo_ref, lnw_ref, w1_ref, w2_ref, lnf_ref, o_ref, u_ref, *, tm, lp, final):
    kk = pl.program_id(1)

    @pl.when(kk == 0)
    def _():
        valid = _row_valid(tm, lp)
        mixed = jnp.where(valid > 0.0, m_ref[...], jnp.zeros((), m_ref.dtype))
        x = h_ref[...] + jnp.dot(mixed, wo_ref[...], preferred_element_type=F32)
        ms = jnp.mean(x * x, axis=-1, keepdims=True)
        u_ref[...] = (x * lax.rsqrt(ms + NORM_EPS) * lnw_ref[...] * valid).astype(BF16)
        o_ref[...] = x

    a = jnp.maximum(jnp.dot(u_ref[...], w1_ref[...], preferred_element_type=F32), 0.0)
    o_ref[...] += jnp.dot((a * a).astype(BF16), w2_ref[...], preferred_element_type=F32)

    if final:
        @pl.when(kk == pl.num_programs(1) - 1)
        def _():
            y = o_ref[...]
            ms = jnp.mean(y * y, axis=-1, keepdims=True)
            o_ref[...] = y * lax.rsqrt(ms + NORM_EPS) * lnf_ref[...]


def _tail(h, mixed, w_out, lnw, w1, w2, lnf, tm, th, lp, final):
    t = h.shape[0]
    return pl.pallas_call(
        functools.partial(_tail_kernel, tm=tm, lp=lp, final=final),
        grid=(t // tm, MLP_HIDDEN // th),
        in_specs=[pl.BlockSpec((tm, D_MODEL), lambda i, k: (i, 0)),
                  pl.BlockSpec((tm, D_MODEL), lambda i, k: (i, 0)),
                  pl.BlockSpec((D_MODEL, D_MODEL), lambda i, k: (0, 0), pipeline_mode=pl.Buffered(1)),
                  pl.BlockSpec((1, D_MODEL), lambda i, k: (0, 0)),
                  pl.BlockSpec((D_MODEL, th), lambda i, k: (0, k)),
                  pl.BlockSpec((th, D_MODEL), lambda i, k: (k, 0)),
                  pl.BlockSpec((1, D_MODEL), lambda i, k: (0, 0))],
        out_specs=pl.BlockSpec((tm, D_MODEL), lambda i, k: (i, 0)),
        out_shape=jax.ShapeDtypeStruct((t, D_MODEL), F32),
        scratch_shapes=[pltpu.VMEM((tm, D_MODEL), BF16)],
        compiler_params=_params(("parallel", "arbitrary")),
        name="out_mlp",
    )(h, mixed, w_out, lnw, w1, w2, lnf)


def _pad_cols(a, width):
    return jnp.pad(a, ((0, 0), (0, width - a.shape[1])))


def _pad_rows(a, height):
    return jnp.pad(a, ((0, height - a.shape[0]), (0, 0)))


def _split_in_cols(a, a_vres):
    o = 0
    hcols = a[:, o:o + H_COLS]; o += H_COLS
    z = a[:, o:o + M_WIDTH]; o += M_WIDTH
    xm = a[:, o:o + M_WIDTH]; o += M_WIDTH
    bc = a[:, o:o + 2 * M_BC]; o += 2 * M_BC
    dt = a[:, o:o + M_HEADS]; o += M_HEADS
    rkv = a[:, o:o + 3 * R_WIDTH]; o += 3 * R_WIDTH
    w1 = a[:, o:o + R_DECAY_RANK]; o += R_DECAY_RANK
    a1 = a[:, o:o + R_AAA_RANK]; o += R_AAA_RANK
    g1 = a[:, o:o + R_GATE_RANK]; o += R_GATE_RANK
    gates = a[:, o:o + GATE_COLS]; o += GATE_COLS
    assert o == IN_COLS
    out = jnp.concatenate([hcols, gates, z, xm, rkv, bc, g1, _pad_cols(w1, LANES), _pad_cols(a1, LANES),
                           _pad_cols(dt, LANES), _pad_cols(a_vres, LANES)], axis=1)
    assert out.shape[1] == P_COLS
    return out


def _tile(t, candidates):
    for c in candidates:
        if t % c == 0:
            return c
    raise ValueError(f"no tile for {t}")


def kernel(x, meta, ln1_w, ln2_w, lnf_w, w_in, w_in_vres, hg_lb_logits, hg_norm_w, m_conv_w, m_conv_b, m_dt_bias, m_a_log, m_d, m_norm_w, r_mu, r_mu_vres, r_w0, r_w2, r_a0, r_a2, r_v0, r_v2, r_g2, r_k_k, r_k_a, r_r_k, r_gn_w, r_gn_b, w_up_h, w_up_m, w_up_r, w_out, w_mlp_in, w_mlp_out):
    batch, seq, d = x.shape
    assert d == D_MODEL and M_HEAD_DIM == CHUNK and R_HEAD_DIM == CHUNK
    l_real = seq + CHUNK
    lp = -(-l_real // ROW_ALIGN) * ROW_ALIGN
    t = batch * lp
    tm_proj = _tile(t, (1280, 640))
    tm = _tile(t, (640,))
    tm_tail = _tile(t, (512, 256))

    h = jnp.concatenate([jnp.zeros((batch, N_PAD, d), x.dtype),
                         jnp.broadcast_to(meta.astype(x.dtype), (batch, N_META, d)),
                         x, jnp.zeros((batch, lp - l_real, d), x.dtype)], axis=1).reshape(t, d)

    v_first = None
    for l in range(DEPTH):
        vres_w = w_in_vres[l - 1] if l > 0 else jnp.zeros((d, R_MV_RANK), w_in.dtype)
        w_comb = _split_in_cols(w_in[l], vres_w).astype(BF16)
        p = _proj(h, ln1_w[l][None, :], w_comb, tm_proj, 1280)

        o_h = _hgrn(p, hg_lb_logits, hg_norm_w[l].reshape(1, H_WIDTH), l, batch, lp)
        o_m = _mamba(p, m_conv_w[l], m_conv_b[l], m_dt_bias[l], m_a_log[l], m_d[l], m_norm_w[l], batch, lp)

        mu = r_mu[l][None, :]
        o = 3 * R_WIDTH
        prm = {
            "mu_rkv": mu[:, :o],
            "mu_wa": jnp.concatenate([_pad_cols(mu[:, o:o + R_DECAY_RANK], LANES),
                                      _pad_cols(mu[:, o + R_DECAY_RANK:o + R_DECAY_RANK + R_AAA_RANK], LANES)], axis=1),
            "mu_g1": mu[:, o + R_DECAY_RANK + R_AAA_RANK:],
            "w0": r_w0[l][None, :], "w2": _pad_rows(r_w2[l], LANES).astype(BF16),
            "a0": r_a0[l][None, :], "a2": _pad_rows(r_a2[l], LANES).astype(BF16),
            "g2": r_g2[l].astype(BF16),
            "k_k": r_k_k[l][None, :], "k_a": r_k_a[l][None, :], "r_k": r_r_k[l].reshape(1, R_WIDTH),
            "gn_w": r_gn_w[l][None, :], "gn_b": r_gn_b[l][None, :],
        }
        if l == 0:
            o_r, v_first = _rwkv(p, None, prm, True, batch, lp)
        else:
            prm["mu_vr"] = _pad_cols(r_mu_vres[l - 1][None, :], LANES)
            prm["v0"] = r_v0[l - 1][None, :]
            prm["v2"] = _pad_rows(r_v2[l - 1], LANES).astype(BF16)
            (o_r,) = _rwkv(p, v_first, prm, False, batch, lp)

        mixed = _merge(o_h, o_m, o_r, p, w_up_h[l].astype(BF16), w_up_m[l].astype(BF16),
                       w_up_r[l].astype(BF16), tm, 1024)
        h = _tail(h, mixed, w_out[l].astype(BF16), ln2_w[l][None, :], w_mlp_in[l].astype(BF16),
                  w_mlp_out[l].astype(BF16), lnf_w[None, :], tm_tail, 1024, lp, l == DEPTH - 1)

    return h.reshape(batch, lp, d)[:, CHUNK:l_real]
```

```python
import functools

import jax
import jax.numpy as jnp
from jax import lax
from jax.experimental import pallas as pl
from jax.experimental.pallas import tpu as pltpu

F32 = jnp.float32
BF16 = jnp.bfloat16

LANES = 128

D_MODEL = 2048
DEPTH = 2
N_META = 16
CHUNK = 64
N_PAD = CHUNK - N_META
MLP_HIDDEN = 4 * D_MODEL
NORM_EPS = 1e-6
N_BRANCH = 3
L2_EPS = 1e-24

H_WIDTH = D_MODEL // 2
H_HEAD_DIM = 128
H_HEADS = H_WIDTH // H_HEAD_DIM

M_WIDTH = D_MODEL // 2
M_HEAD_DIM = 64
M_HEADS = M_WIDTH // M_HEAD_DIM
M_GROUPS = 2
M_STATE = 128
M_CONV = 4
M_BC = M_GROUPS * M_STATE

R_WIDTH = D_MODEL // 2
R_HEAD_DIM = 64
R_HEADS = R_WIDTH // R_HEAD_DIM
R_DECAY_RANK = max(32, int(round(1.8 * D_MODEL ** 0.5 / 32)) * 32)
R_AAA_RANK = max(32, int(round(1.8 * D_MODEL ** 0.5 / 32)) * 32)
R_MV_RANK = max(32, int(round(1.3 * D_MODEL ** 0.5 / 32)) * 32)
R_GATE_RANK = max(32, int(round(0.6 * D_MODEL ** 0.8 / 32)) * 32)
R_GN_EPS = 64e-5
N_PAIRS = R_HEADS // 2

H_COLS = 4 * H_WIDTH
M_CONV_CH = M_WIDTH + 2 * M_BC
M_COLS = M_WIDTH + M_CONV_CH + M_HEADS
R_COLS = 3 * R_WIDTH + R_DECAY_RANK + R_AAA_RANK + R_GATE_RANK
GATE_COLS = N_BRANCH * D_MODEL
IN_COLS = H_COLS + M_COLS + R_COLS + GATE_COLS

C_H = 0
C_GATE = C_H + H_COLS
C_Z = C_GATE + GATE_COLS
C_X = C_Z + M_WIDTH
C_RKV = C_X + M_WIDTH
C_BC = C_RKV + 3 * R_WIDTH
C_G1 = C_BC + 2 * M_BC
C_WA = C_G1 + R_GATE_RANK
C_DT = C_WA + 2 * LANES
C_VRES = C_DT + LANES
P_COLS = C_VRES + LANES

ROW_ALIGN = 1280
SEQ_BLOCK = 256
HG_BLOCK = 640
VMEM_LIMIT = 56 * 1024 * 1024

NN = (((1,), (0,)), ((), ()))
NT = (((1,), (1,)), ((), ()))
TN = (((0,), (0,)), ((), ()))


def _dot(a, b, dims=NN):
    return lax.dot_general(a.astype(BF16), b.astype(BF16), dims, preferred_element_type=F32)


def _split(x, terms):
    out = []
    for _ in range(terms - 1):
        hi = x.astype(BF16)
        out.append(hi)
        x = x - hi.astype(F32)
    out.append(x.astype(BF16))
    return out


def _mask_dot_rhs(m, x, terms=3):
    mb = m.astype(BF16)
    return jnp.dot(jnp.concatenate([mb] * terms, axis=1), jnp.concatenate(_split(x, terms), axis=0),
                   preferred_element_type=F32)


def _mask_dot_lhs(x, m, terms=3):
    mb = m.astype(BF16)
    return jnp.dot(jnp.concatenate(_split(x, terms), axis=1), jnp.concatenate([mb] * terms, axis=0),
                   preferred_element_type=F32)


LOG2E = 1.4426950408889634


def _sigmoid(x):
    return 1.0 / (1.0 + jnp.exp2(x * -LOG2E))


def _silu(x):
    return x * _sigmoid(x)


def _softplus(x):
    return jnp.maximum(x, 0.0) + jnp.log(1.0 + jnp.exp(-jnp.abs(x)))


def _iota(shape, dim):
    return lax.broadcasted_iota(jnp.int32, shape, dim)


def _tri(n):
    return (_iota((n, n), 0) >= _iota((n, n), 1)).astype(F32)


def _head_masks(width=LANES):
    lane = _iota((CHUNK, width), 1)
    return [lane // CHUNK == hd for hd in range(width // CHUNK)]


def _block_diag(x, masks):
    return jnp.concatenate([jnp.where(m_, x, 0.0) for m_ in masks], axis=0)


def _params(sem):
    return pltpu.CompilerParams(dimension_semantics=sem, vmem_limit_bytes=VMEM_LIMIT)


def _proj_kernel(h_ref, lnw_ref, w_ref, o_ref, u_ref):
    @pl.when(pl.program_id(1) == 0)
    def _():
        x = h_ref[...]
        ms = jnp.mean(x * x, axis=-1, keepdims=True)
        u_ref[...] = (x * lax.rsqrt(ms + NORM_EPS) * lnw_ref[...]).astype(BF16)

    o_ref[...] = jnp.dot(u_ref[...], w_ref[...], preferred_element_type=F32).astype(o_ref.dtype)


def _proj(h, lnw, w, tm, tn):
    t = h.shape[0]
    return pl.pallas_call(
        _proj_kernel,
        grid=(t // tm, P_COLS // tn),
        in_specs=[
            pl.BlockSpec((tm, D_MODEL), lambda i, j: (i, 0)),
            pl.BlockSpec((1, D_MODEL), lambda i, j: (0, 0)),
            pl.BlockSpec((D_MODEL, tn), lambda i, j: (0, j)),
        ],
        out_specs=pl.BlockSpec((tm, tn), lambda i, j: (i, j)),
        out_shape=jax.ShapeDtypeStruct((t, P_COLS), BF16),
        scratch_shapes=[pltpu.VMEM((tm, D_MODEL), BF16)],
        compiler_params=_params(("parallel", "arbitrary")),
        name="in_proj",
    )(h, lnw, w)


HG_SUB = 8
HG_UNROLL = 5


def _hgrn_kernel(q_ref, f_ref, i_ref, g_ref, lbl_ref, nw_ref, o_ref,
                 st_ref, *, layer, rb):
    jb = pl.program_id(1)

    @pl.when(jb == 0)
    def _():
        st_ref[...] = jnp.zeros(st_ref.shape, F32)

    lg = lbl_ref[...]
    mx = jnp.max(lg, axis=0, keepdims=True)
    ex = jnp.exp(lg - mx)
    sm = ex / jnp.sum(ex, axis=0, keepdims=True)
    lb = jnp.sum(sm[0:layer + 1], axis=0, keepdims=True) - sm[0:1]

    assert HG_SUB == 8
    col8 = _iota((HG_SUB, H_HEAD_DIM), 1)
    key_row = [(_iota((HG_SUB, H_HEAD_DIM), 0) + (HG_SUB - d)) % HG_SUB for d in range(HG_SUB)]
    heads = [slice(hd * H_HEAD_DIM, (hd + 1) * H_HEAD_DIM) for hd in range(H_HEADS)]
    arow = _iota((CHUNK, CHUNK), 0)
    acol = _iota((CHUNK, CHUNK), 1)
    off_mask = acol < (arow // HG_SUB) * HG_SUB
    causal = arow >= acol
    n_sub = CHUNK // HG_SUB

    tri = _tri(CHUNK)
    row = _iota((CHUNK, 1), 0)

    def one_chunk(r0):
        rows = pl.ds(r0, CHUNK)
        valid = ((jb * rb + r0 + row) >= N_PAD).astype(F32)
        q = _silu(q_ref[rows, :].astype(F32))
        fr = f_ref[rows, :].astype(F32)
        sig = _sigmoid(fr)
        k = (1.0 - lb) * (1.0 - sig) * valid
        v = i_ref[rows, :].astype(F32)
        g2 = _mask_dot_rhs(tri, jnp.log(lb + (1.0 - lb) * sig)) * LOG2E
        hk = g2 - jnp.log(k) * LOG2E

        diag = [[None] * n_sub for _ in heads]
        for sub in range(n_sub):
            base = sub * HG_SUB
            sub_rows = slice(base, base + HG_SUB)
            g_i, q_i, k_i, h_i = g2[sub_rows], q[sub_rows], k[sub_rows], hk[sub_rows]
            acc = [jnp.zeros((HG_SUB, H_HEAD_DIM), F32) for _ in heads]
            for d in range(HG_SUB):
                if d == 0:
                    e = q_i * k_i
                else:
                    e = jnp.exp2(jnp.minimum(g_i - pltpu.roll(h_i, d, axis=0), 0.0)) * q_i
                at_key = col8 == key_row[d] + base
                for hd, hs in enumerate(heads):
                    s = jnp.sum(e[:, hs], axis=-1, keepdims=True)
                    acc[hd] = jnp.where(at_key, s, acc[hd])
            for hd in range(H_HEADS):
                diag[hd][sub] = acc[hd]

        g_last = g2[CHUNK - 1:CHUNK, :]
        qg = q * jnp.exp2(g2)
        kdec = jnp.exp2(g_last - hk)
        e_last = jnp.exp2(g_last)
        attn = []
        for hd, hs in enumerate(heads):
            g_h, q_h, hk_h = g2[:, hs], q[:, hs], hk[:, hs]
            qcat, kcat = [], []
            for sub in range(1, n_sub):
                base = sub * HG_SUB
                g_r = g_h[base:base + 1, :]
                q_sub = q_h[base:base + HG_SUB] * jnp.exp2(jnp.minimum(g_h[base:base + HG_SUB] - g_r, 0.0))
                pieces = [jnp.zeros((base, H_HEAD_DIM), F32), q_sub]
                if base + HG_SUB < CHUNK:
                    pieces.append(jnp.zeros((CHUNK - base - HG_SUB, H_HEAD_DIM), F32))
                qcat.append(jnp.concatenate(pieces, axis=0))
                k_sub = jnp.exp2(jnp.minimum(g_r - hk_h[0:base], 0.0))
                kcat.append(jnp.concatenate([k_sub, jnp.zeros((CHUNK - base, H_HEAD_DIM), F32)], axis=0))
            a_off = _dot(jnp.concatenate(qcat, axis=1), jnp.concatenate(kcat, axis=1), NT)
            a_diag = jnp.concatenate(diag[hd], axis=0)[:, 0:CHUNK]
            attn.append(jnp.where(off_mask, a_off, jnp.where(causal, a_diag, 0.0)))
        sts = [st_ref[hd] for hd in range(H_HEADS)]
        inter = [_dot(qg[:, hs], sts[hd], NT) for hd, hs in enumerate(heads)]
        upd = [_dot(v[:, hs], kdec[:, hs], TN) for hs in heads]
        intra = [_dot(attn[hd], v[:, hs]) for hd, hs in enumerate(heads)]
        outs = []
        for hd, hs in enumerate(heads):
            st_ref[hd] = sts[hd] * e_last[:, hs] + upd[hd]
            o_h = (intra[hd] + inter[hd]) * _sigmoid(g_ref[rows, hs].astype(F32))
            outs.append(o_h * lax.rsqrt(jnp.mean(o_h * o_h, axis=-1, keepdims=True) + NORM_EPS))
        o_ref[rows, :] = (jnp.concatenate(outs, axis=-1) * nw_ref[...]).astype(o_ref.dtype)

    def chunks(cc, carry):
        for u in range(HG_UNROLL):
            one_chunk(pl.multiple_of((cc * HG_UNROLL + u) * CHUNK, CHUNK))
        return carry

    assert (rb // CHUNK) % HG_UNROLL == 0
    lax.fori_loop(0, rb // CHUNK // HG_UNROLL, chunks, 0)


def _hgrn(p, lb_logits, norm_w, layer, batch, lp):
    rb = HG_BLOCK
    nb = lp // rb
    t = p.shape[0]
    cb = C_H // H_WIDTH

    def col(k):
        return pl.BlockSpec((rb, H_WIDTH), lambda b, j, k=k: (b * nb + j, cb + k))

    return pl.pallas_call(
        functools.partial(_hgrn_kernel, layer=layer, rb=rb),
        grid=(batch, nb),
        in_specs=[col(0), col(1), col(2), col(3),
                  pl.BlockSpec((DEPTH, H_WIDTH), lambda b, j: (0, 0)),
                  pl.BlockSpec((1, H_WIDTH), lambda b, j: (0, 0))],
        out_specs=pl.BlockSpec((rb, H_WIDTH), lambda b, j: (b * nb + j, 0)),
        out_shape=jax.ShapeDtypeStruct((t, H_WIDTH), BF16),
        scratch_shapes=[pltpu.VMEM((H_HEADS, H_HEAD_DIM, H_HEAD_DIM), F32)],
        compiler_params=_params(("parallel", "arbitrary")),
        name="hgrn2",
    )(p, p, p, p, lb_logits, norm_w)


HIST = 8


def _mamba_kernel(z_ref, x_ref, bc_ref, dt_ref, cwx_ref, cwb_ref, cbx_ref, cbb_ref,
                  dtb_ref, alog_ref, dsk_ref, nw_ref, expand_ref, tri_ref, ones_ref, le_ref, ge_ref, o_ref,
                  st_ref, xe, be, xa, ba, xc_s, da_s, *, rb):
    jb = pl.program_id(1)

    @pl.when(jb == 0)
    def _():
        st_ref[...] = jnp.zeros(st_ref.shape, F32)
        xe[0:HIST, :] = jnp.zeros((HIST, M_WIDTH), F32)
        be[0:HIST, :] = jnp.zeros((HIST, 2 * M_BC), F32)

    def conv(ext, src_ref, w_ref, b_ref, dst):
        ext[HIST:HIST + rb, :] = src_ref[...].astype(F32)
        full = ext[...]
        acc = b_ref[...] + w_ref[M_CONV - 1:M_CONV, :] * full[HIST:HIST + rb]
        for d in range(1, M_CONV):
            acc = acc + w_ref[M_CONV - 1 - d:M_CONV - d, :] * pltpu.roll(full, d, axis=0)[HIST:HIST + rb]
        dst[...] = _silu(acc)
        ext[0:HIST, :] = ext[rb:rb + HIST, :]

    conv(xe, x_ref, cwx_ref, cbx_ref, xa)
    conv(be, bc_ref, cwb_ref, cbb_ref, ba)

    expand = expand_ref[...]
    valid = ((jb * rb + _iota((rb, 1), 0)) >= N_PAD).astype(F32)
    dt = _softplus(dt_ref[...].astype(F32) + dtb_ref[...]) * valid
    da_s[...] = _mask_dot_lhs(dt * -jnp.exp(alog_ref[...]), expand)
    xc_s[...] = xa[...] * _mask_dot_lhs(dt, expand, terms=2)
    hpg = M_HEADS // M_GROUPS
    hm = _head_masks()

    sts = [st_ref[pr] for pr in range(M_HEADS // 2)]
    for c in range(rb // CHUNK):
        rows = slice(c * CHUNK, (c + 1) * CHUNK)
        xs = xa[rows, :]
        bcm = ba[rows, :]
        da_e = da_s[rows, :]
        acum = _mask_dot_rhs(tri_ref[...], da_e)
        acum_j = _mask_dot_rhs(ones_ref[...], da_e * le_ref[...])
        lmat = jnp.exp(jnp.minimum(acum - acum_j, 0.0)) * ge_ref[...]
        a_last = acum[CHUNK - 1:CHUNK, :]
        e_cum = jnp.exp(acum)
        e_end = jnp.exp(a_last - acum)
        e_last = jnp.exp(a_last)
        xc = xc_s[rows, :]
        scores = []
        for g in range(M_GROUPS):
            b_g = bcm[:, g * M_STATE:(g + 1) * M_STATE]
            c_g = bcm[:, M_BC + g * M_STATE:M_BC + (g + 1) * M_STATE]
            scores.append(_dot(c_g, jnp.concatenate([b_g] * hpg, axis=0), NT))
        attn = jnp.concatenate(scores, axis=1) * lmat
        ys = []
        for pr in range(M_HEADS // 2):
            ps = slice(pr * LANES, (pr + 1) * LANES)
            g = (2 * pr) // hpg
            b_g = bcm[:, g * M_STATE:(g + 1) * M_STATE]
            c_g = bcm[:, M_BC + g * M_STATE:M_BC + (g + 1) * M_STATE]
            xc_p = xc[:, ps]
            y = _dot(attn[:, ps], _block_diag(xc_p, hm)) + _dot(c_g, sts[pr]) * e_cum[:, ps]
            sts[pr] = sts[pr] * e_last[:, ps] + _dot(b_g, xc_p * e_end[:, ps], TN)
            ys.append(y)
        y = (jnp.concatenate(ys, axis=1) + dsk_ref[...] * xs) * _silu(z_ref[rows, :].astype(F32))
        gw = M_WIDTH // M_GROUPS
        outs = []
        for g in range(M_GROUPS):
            yg = y[:, g * gw:(g + 1) * gw]
            outs.append(yg * lax.rsqrt(jnp.mean(yg * yg, axis=-1, keepdims=True) + NORM_EPS))
        o_ref[rows, :] = (jnp.concatenate(outs, axis=1) * nw_ref[...]).astype(o_ref.dtype)
    for pr in range(M_HEADS // 2):
        st_ref[pr] = sts[pr]


def _mamba(p, conv_w, conv_b, dt_bias, a_log, d_skip, norm_w, batch, lp):
    rb = SEQ_BLOCK
    nb = lp // rb
    t = p.shape[0]

    def col(off, width):
        return pl.BlockSpec((rb, width), lambda b, j: (b * nb + j, off // width))

    def full(shape):
        return pl.BlockSpec(shape, lambda b, j: (0,) * len(shape))

    pad = LANES - M_HEADS
    dtb = jnp.pad(dt_bias.reshape(1, M_HEADS), ((0, 0), (0, pad)))
    alog = jnp.pad(a_log.reshape(1, M_HEADS), ((0, 0), (0, pad)))
    dsk = jnp.repeat(d_skip, M_HEAD_DIM).reshape(1, M_WIDTH)
    expand = (_iota((LANES, M_WIDTH), 1) // M_HEAD_DIM == _iota((LANES, M_WIDTH), 0)).astype(BF16)
    row_i = _iota((CHUNK, M_WIDTH), 0)
    pos_j = _iota((CHUNK, M_WIDTH), 1) % CHUNK
    consts = [expand, _tri(CHUNK).astype(BF16), jnp.ones((CHUNK, CHUNK), BF16),
              (row_i <= pos_j).astype(F32), (row_i >= pos_j).astype(F32)]
    return pl.pallas_call(
        functools.partial(_mamba_kernel, rb=rb),
        grid=(batch, nb),
        in_specs=[col(C_Z, M_WIDTH), col(C_X, M_WIDTH), col(C_BC, 2 * M_BC), col(C_DT, LANES),
                  full((M_CONV, M_WIDTH)), full((M_CONV, 2 * M_BC)),
                  full((1, M_WIDTH)), full((1, 2 * M_BC)),
                  full((1, LANES)), full((1, LANES)), full((1, M_WIDTH)), full((1, M_WIDTH))]
        + [full(c.shape) for c in consts],
        out_specs=pl.BlockSpec((rb, M_WIDTH), lambda b, j: (b * nb + j, 0)),
        out_shape=jax.ShapeDtypeStruct((t, M_WIDTH), BF16),
        scratch_shapes=[pltpu.VMEM((M_HEADS // 2, M_STATE, LANES), F32),
                        pltpu.VMEM((rb + HIST, M_WIDTH), F32),
                        pltpu.VMEM((rb + HIST, 2 * M_BC), F32),
                        pltpu.VMEM((rb, M_WIDTH), F32),
                        pltpu.VMEM((rb, 2 * M_BC), F32)]
        + [pltpu.VMEM((rb, M_WIDTH), F32) for _ in range(2)],
        compiler_params=_params(("parallel", "arbitrary")),
        name="mamba2",
    )(p, p, p, p, conv_w[:, :M_WIDTH], conv_w[:, M_WIDTH:], conv_b[None, :M_WIDTH],
      conv_b[None, M_WIDTH:], dtb, alog, dsk, norm_w[None, :], *consts)


R_SHIFT_W = 3 * R_WIDTH
R_PRE = 2
R_SLAB = LANES


def _rwkv_kernel(*refs, first, rb):
    if first:
        (rkv_ref, g1_ref, wa_ref, mu_rkv, mu_g1, mu_wa,
         w0_ref, w2_ref, a0_ref, a2_ref, g2_ref, kk_ref, ka_ref, rk_ref, gnw_ref, gnb_ref,
         o_ref, vf_out,
         st_ref, e_rkv, e_g1, e_wa, dec_s,
         lw_s, r_s, k_s, v_s, a_s, b_s, g_s, rt_s, wt_s, uv_s, arb_s, ov_s, pm_s, cm_s) = refs
    else:
        (rkv_ref, g1_ref, wa_ref, vr_ref, vf_ref, mu_rkv, mu_g1, mu_wa, mu_vr,
         w0_ref, w2_ref, a0_ref, a2_ref, g2_ref, kk_ref, ka_ref, rk_ref, gnw_ref, gnb_ref,
         v0_ref, v2_ref,
         o_ref,
         st_ref, e_rkv, e_g1, e_wa, e_vr, dec_s,
         lw_s, r_s, k_s, v_s, a_s, b_s, g_s, rt_s, wt_s, uv_s, arb_s, ov_s, pm_s, cm_s) = refs
    jb = pl.program_id(1)
    exts = [e_rkv, e_g1, e_wa] + ([] if first else [e_vr])

    @pl.when(jb == 0)
    def _():
        st_ref[...] = jnp.zeros(st_ref.shape, F32)
        for ext in exts:
            ext[0:HIST, :] = jnp.zeros((HIST, ext.shape[1]), F32)

    def shift(ext, src_ref, mu_ref):
        ext[HIST:HIST + rb, :] = src_ref[...].astype(F32)
        full = ext[...]
        cur = full[HIST:HIST + rb]
        prev = pltpu.roll(full, 1, axis=0)[HIST:HIST + rb]
        out = cur + (prev - cur) * mu_ref[...]
        ext[0:HIST, :] = ext[rb:rb + HIST, :]
        return out

    rkv = shift(e_rkv, rkv_ref, mu_rkv)
    gl = shift(e_g1, g1_ref, mu_g1)
    wa = shift(e_wa, wa_ref, mu_wa)
    r = rkv[:, 0:R_WIDTH]
    k = rkv[:, R_WIDTH:2 * R_WIDTH]
    v = rkv[:, 2 * R_WIDTH:3 * R_WIDTH]
    wl = wa[:, 0:LANES]
    al = wa[:, LANES:2 * LANES]

    valid = ((jb * rb + _iota((rb, 1), 0)) >= N_PAD).astype(F32)
    w_log = -_softplus(-(w0_ref[...] + _dot(jnp.tanh(wl), w2_ref[...]))) - 0.5
    lw_s[...] = -jnp.exp(w_log)
    a = _sigmoid(a0_ref[...] + _dot(al, a2_ref[...]))
    if first:
        vf_out[...] = v
    else:
        vl = shift(e_vr, vr_ref, mu_vr)
        v = v + (vf_ref[...] - v) * _sigmoid(v0_ref[...] + _dot(vl, v2_ref[...]))
    g_s[...] = _dot(_sigmoid(gl), g2_ref[...])

    seg_ones = (_iota((LANES, LANES), 0) // R_HEAD_DIM == _iota((LANES, LANES), 1) // R_HEAD_DIM).astype(F32)

    def head_sum(x):
        return jnp.concatenate(
            [_mask_dot_lhs(x[:, s * LANES:(s + 1) * LANES], seg_ones, terms=2) for s in range(N_PAIRS)], axis=1)

    kk = k * kk_ref[...]
    kk = kk * lax.rsqrt(jnp.maximum(head_sum(kk * kk), L2_EPS))
    kh = k * (1.0 + (a - 1.0) * ka_ref[...]) * valid
    r_s[...] = r
    k_s[...] = kh
    v_s[...] = v
    a_s[...] = -kk
    b_s[...] = kk * a

    tri = _tri(CHUNK)
    sl = R_SLAB
    t_i = _iota((CHUNK, sl), 0)
    s_j = _iota((CHUNK, sl), 1) % CHUNK
    strict = t_i > s_j
    incl = t_i >= s_j
    eye = (t_i == s_j).astype(F32)
    same_head = _iota((sl, sl), 0) // R_HEAD_DIM == _iota((sl, sl), 1) // R_HEAD_DIM
    n_double = 5
    assert 2 ** (n_double + 1) == CHUNK

    n_slabs = R_WIDTH // sl
    pairs = [slice(i * sl, (i + 1) * sl) for i in range(n_slabs)]
    hm = _head_masks(sl)

    def precompute(cc):
        units = []
        for sub in range(R_PRE):
            c = cc * R_PRE + sub
            rows = slice(c * CHUNK, (c + 1) * CHUNK)
            lw = lw_s[rows, :]
            cum = _mask_dot_rhs(tri, lw)
            c_last = cum[CHUNK - 1:CHUNK, :]
            e_inv = jnp.exp(-cum)
            e_end = jnp.exp(c_last - cum)
            b_c, k_c, v_c = b_s[rows, :], k_s[rows, :], v_s[rows, :]
            a_t = a_s[rows, :] * jnp.exp(cum - lw)
            r_t = r_s[rows, :] * jnp.exp(cum)
            b_t = b_c * e_inv
            k_t = k_c * e_inv
            b_h = b_c * e_end
            k_h = k_c * e_end
            rt_s[rows, :] = r_t
            dec_s[c:c + 1, :] = jnp.exp(c_last)
            units += [(rows, ps, a_t[:, ps], r_t[:, ps], b_t[:, ps], k_t[:, ps], v_c[:, ps],
                       b_h[:, ps], k_h[:, ps], c * n_slabs + pr) for pr, ps in enumerate(pairs)]
        m = [_dot(jnp.concatenate([a_t, r_t], axis=0),
                  jnp.concatenate([_block_diag(b_t, hm), _block_diag(k_t, hm)], axis=0), NT)
             for (_, _, a_t, r_t, b_t, k_t, _, _, _, _) in units]
        a_ab = [jnp.where(strict, x[0:CHUNK, 0:sl], 0.0) for x in m]
        a_ak = [jnp.where(strict, x[0:CHUNK, sl:2 * sl], 0.0) for x in m]
        a_rk = [jnp.where(incl, x[CHUNK:2 * CHUNK, sl:2 * sl], 0.0) for x in m]
        for x, un in zip(m, units):
            arb_s[un[0], un[1]] = jnp.where(incl, x[CHUNK:2 * CHUNK, 0:sl], 0.0)
        inv = [eye + x for x in a_ab]
        pw = [_dot(x, _block_diag(x, hm)) for x in a_ab]
        for it in range(n_double - 1):
            both = [_dot(jnp.concatenate([p_, i_], axis=0), _block_diag(p_, hm)) for p_, i_ in zip(pw, inv)]
            pw = [x[0:CHUNK] for x in both]
            inv = [i_ + x[CHUNK:2 * CHUNK] for i_, x in zip(inv, both)]
        inv = [i_ + _dot(i_, _block_diag(p_, hm)) for i_, p_ in zip(inv, pw)]
        t1ov = [_dot(jnp.concatenate([x, y], axis=0), _block_diag(un[6], hm))
                for x, y, un in zip(a_ak, a_rk, units)]
        wtuv = [_dot(i_, jnp.concatenate([_block_diag(un[2], hm), _block_diag(x[0:CHUNK], hm)], axis=1))
                for i_, un, x in zip(inv, units, t1ov)]
        wt = [x[:, 0:sl] for x in wtuv]
        uv = [x[:, sl:2 * sl] for x in wtuv]
        pm = [_dot(w_, un[7], TN) for w_, un in zip(wt, units)]
        cm = [_dot(jnp.concatenate([x, un[6]], axis=0), jnp.concatenate([un[7], un[8]], axis=0), TN)
              for x, un in zip(uv, units)]
        for i, un in enumerate(units):
            wt_s[un[0], un[1]] = wt[i]
            uv_s[un[0], un[1]] = uv[i]
            ov_s[un[0], un[1]] = t1ov[i][CHUNK:2 * CHUNK]
            pm_s[un[9]] = jnp.where(same_head, pm[i], 0.0)
            cm_s[un[9]] = jnp.where(same_head, cm[i], 0.0)

    n_chunks = rb // CHUNK
    assert n_chunks % R_PRE == 0
    for cc in range(n_chunks // R_PRE):
        precompute(cc)

    inv_n = 1.0 / R_HEAD_DIM

    def finish(rows, o):
        mu = head_sum(o) * inv_n
        d = o - mu
        var = head_sum(d * d) * inv_n
        o = d * lax.rsqrt(var + R_GN_EPS) * gnw_ref[...] + gnb_ref[...]
        o = o + head_sum(r_s[rows, :] * k_s[rows, :] * rk_ref[...]) * v_s[rows, :]
        o_ref[rows, :] = (o * g_s[rows, :]).astype(o_ref.dtype)

    sts = [st_ref[pr] for pr in range(n_slabs)]
    pending = None
    for c in range(n_chunks):
        rows = slice(c * CHUNK, (c + 1) * CHUNK)
        dec = dec_s[c:c + 1, :]
        new_sts = [st * dec[:, ps] + _dot(st, pm_s[c * n_slabs + pr]) + cm_s[c * n_slabs + pr]
                   for pr, (ps, st) in enumerate(zip(pairs, sts))]
        uo = [_dot(jnp.concatenate([wt_s[rows, ps], rt_s[rows, ps]], axis=0), st, NT)
              for ps, st in zip(pairs, sts)]
        u = [x[0:CHUNK] + uv_s[rows, ps] for x, ps in zip(uo, pairs)]
        o1 = [x[CHUNK:2 * CHUNK] for x in uo]
        o2 = [_dot(arb_s[rows, ps], _block_diag(u_, hm)) for ps, u_ in zip(pairs, u)]
        sts = new_sts
        if pending is not None:
            finish(*pending)
        pending = (rows, jnp.concatenate(
            [o1[pr] + o2[pr] + ov_s[rows, ps] for pr, ps in enumerate(pairs)], axis=1))
    finish(*pending)
    for pr in range(n_slabs):
        st_ref[pr] = sts[pr]


def _rwkv(p, v_first, prm, first, batch, lp):
    rb = SEQ_BLOCK
    nb = lp // rb
    t = p.shape[0]

    def col(off, width):
        return pl.BlockSpec((rb, width), lambda b, j: (b * nb + j, off // width))

    def full(a):
        return pl.BlockSpec(a.shape, lambda b, j: (0,) * a.ndim)

    row_spec = pl.BlockSpec((rb, R_WIDTH), lambda b, j: (b * nb + j, 0))
    acts = [p, p, p]
    act_specs = [col(C_RKV, R_SHIFT_W), col(C_G1, R_GATE_RANK), col(C_WA, 2 * LANES)]
    mus = [prm["mu_rkv"], prm["mu_g1"], prm["mu_wa"]]
    tail = []
    if not first:
        acts += [p, v_first]
        act_specs += [col(C_VRES, LANES), row_spec]
        mus.append(prm["mu_vr"])
        tail = [prm["v0"], prm["v2"]]
    consts = mus + [prm[n] for n in ("w0", "w2", "a0", "a2", "g2", "k_k", "k_a", "r_k", "gn_w", "gn_b")] + tail
    out_shape = [jax.ShapeDtypeStruct((t, R_WIDTH), BF16)]
    out_specs = [row_spec]
    if first:
        out_shape.append(jax.ShapeDtypeStruct((t, R_WIDTH), F32))
        out_specs.append(row_spec)
    ext_w = [R_SHIFT_W, R_GATE_RANK, 2 * LANES] + ([] if first else [LANES])
    n_slabs = R_WIDTH // R_SLAB
    scratch = ([pltpu.VMEM((n_slabs, R_SLAB, R_SLAB), F32)]
               + [pltpu.VMEM((rb + HIST, w), F32) for w in ext_w]
               + [pltpu.VMEM((max(rb // CHUNK, 8), R_WIDTH), F32)]
               + [pltpu.VMEM((rb, R_WIDTH), F32) for _ in range(12)]
               + [pltpu.VMEM((rb // CHUNK * n_slabs, R_SLAB, R_SLAB), F32) for _ in range(2)])
    return pl.pallas_call(
        functools.partial(_rwkv_kernel, first=first, rb=rb),
        grid=(batch, nb),
        in_specs=act_specs + [full(a) for a in consts],
        out_specs=out_specs,
        out_shape=out_shape,
        scratch_shapes=scratch,
        compiler_params=_params(("parallel", "arbitrary")),
        name="rwkv7",
    )(*acts, *consts)


def _merge_kernel(oh_ref, om_ref, or_ref, g0_ref, g1_ref, g2_ref, wh_ref, wm_ref, wr_ref, o_ref):
    def gated(g_ref, x_ref, w_ref):
        return _sigmoid(g_ref[...].astype(F32)) * jnp.dot(x_ref[...], w_ref[...], preferred_element_type=F32)

    acc = gated(g0_ref, oh_ref, wh_ref) + gated(g1_ref, om_ref, wm_ref) + gated(g2_ref, or_ref, wr_ref)
    o_ref[...] = acc.astype(o_ref.dtype)


def _merge(o_h, o_m, o_r, p, w_h, w_m, w_r, tm, tn):
    t = p.shape[0]
    gb = C_GATE // tn
    nt = D_MODEL // tn

    def act():
        return pl.BlockSpec((tm, H_WIDTH), lambda i, j: (i, 0))

    def gate(k):
        return pl.BlockSpec((tm, tn), lambda i, j, k=k: (i, gb + k * nt + j))

    def wt():
        return pl.BlockSpec((H_WIDTH, tn), lambda i, j: (0, j))

    return pl.pallas_call(
        _merge_kernel,
        grid=(t // tm, nt),
        in_specs=[act(), act(), act(), gate(0), gate(1), gate(2), wt(), wt(), wt()],
        out_specs=pl.BlockSpec((tm, tn), lambda i, j: (i, j)),
        out_shape=jax.ShapeDtypeStruct((t, D_MODEL), BF16),
        compiler_params=_params(("parallel", "arbitrary")),
        name="merge",
    )(o_h, o_m, o_r, p, p, p, w_h, w_m, w_r)


def _row_valid(tm, lp):
    pos = (pl.program_id(0) * tm + _iota((tm, 1), 0)) % lp
    return (pos >= N_PAD).astype(F32)


def _tail_kernel(h_ref, m_ref, wo_ref, lnw_ref, w1_ref, w2_ref, lnf_ref, o_ref, u_ref, *, tm, lp, final):
    kk = pl.program_id(1)

    @pl.when(kk == 0)
    def _():
        valid = _row_valid(tm, lp)
        mixed = jnp.where(valid > 0.0, m_ref[...], jnp.zeros((), m_ref.dtype))
        x = h_ref[...] + jnp.dot(mixed, wo_ref[...], preferred_element_type=F32)
        ms = jnp.mean(x * x, axis=-1, keepdims=True)
        u_ref[...] = (x * lax.rsqrt(ms + NORM_EPS) * lnw_ref[...] * valid).astype(BF16)
        o_ref[...] = x

    a = jnp.maximum(jnp.dot(u_ref[...], w1_ref[...], preferred_element_type=F32), 0.0)
    o_ref[...] += jnp.dot((a * a).astype(BF16), w2_ref[...], preferred_element_type=F32)

    if final:
        @pl.when(kk == pl.num_programs(1) - 1)
        def _():
            y = o_ref[...]
            ms = jnp.mean(y * y, axis=-1, keepdims=True)
            o_ref[...] = y * lax.rsqrt(ms + NORM_EPS) * lnf_ref[...]


def _tail(h, mixed, w_out, lnw, w1, w2, lnf, tm, th, lp, final):
    t = h.shape[0]
    return pl.pallas_call(
        functools.partial(_tail_kernel, tm=tm, lp=lp, final=final),
        grid=(t // tm, MLP_HIDDEN // th),
        in_specs=[pl.BlockSpec((tm, D_MODEL), lambda i, k: (i, 0)),
                  pl.BlockSpec((tm, D_MODEL), lambda i, k: (i, 0)),
                  pl.BlockSpec((D_MODEL, D_MODEL), lambda i, k: (0, 0), pipeline_mode=pl.Buffered(1)),
                  pl.BlockSpec((1, D_MODEL), lambda i, k: (0, 0)),
                  pl.BlockSpec((D_MODEL, th), lambda i, k: (0, k)),
                  pl.BlockSpec((th, D_MODEL), lambda i, k: (k, 0)),
                  pl.BlockSpec((1, D_MODEL), lambda i, k: (0, 0))],
        out_specs=pl.BlockSpec((tm, D_MODEL), lambda i, k: (i, 0)),
        out_shape=jax.ShapeDtypeStruct((t, D_MODEL), F32),
        scratch_shapes=[pltpu.VMEM((tm, D_MODEL), BF16)],
        compiler_params=_params(("parallel", "arbitrary")),
        name="out_mlp",
    )(h, mixed, w_out, lnw, w1, w2, lnf)


def _pad_cols(a, width):
    return jnp.pad(a, ((0, 0), (0, width - a.shape[1])))


def _pad_rows(a, height):
    return jnp.pad(a, ((0, height - a.shape[0]), (0, 0)))


def _split_in_cols(a, a_vres):
    o = 0
    hcols = a[:, o:o + H_COLS]; o += H_COLS
    z = a[:, o:o + M_WIDTH]; o += M_WIDTH
    xm = a[:, o:o + M_WIDTH]; o += M_WIDTH
    bc = a[:, o:o + 2 * M_BC]; o += 2 * M_BC
    dt = a[:, o:o + M_HEADS]; o += M_HEADS
    rkv = a[:, o:o + 3 * R_WIDTH]; o += 3 * R_WIDTH
    w1 = a[:, o:o + R_DECAY_RANK]; o += R_DECAY_RANK
    a1 = a[:, o:o + R_AAA_RANK]; o += R_AAA_RANK
    g1 = a[:, o:o + R_GATE_RANK]; o += R_GATE_RANK
    gates = a[:, o:o + GATE_COLS]; o += GATE_COLS
    assert o == IN_COLS
    out = jnp.concatenate([hcols, gates, z, xm, rkv, bc, g1, _pad_cols(w1, LANES), _pad_cols(a1, LANES),
                           _pad_cols(dt, LANES), _pad_cols(a_vres, LANES)], axis=1)
    assert out.shape[1] == P_COLS
    return out


def _tile(t, candidates):
    for c in candidates:
        if t % c == 0:
            return c
    raise ValueError(f"no tile for {t}")


def kernel(x, meta, ln1_w, ln2_w, lnf_w, w_in, w_in_vres, hg_lb_logits, hg_norm_w, m_conv_w, m_conv_b, m_dt_bias, m_a_log, m_d, m_norm_w, r_mu, r_mu_vres, r_w0, r_w2, r_a0, r_a2, r_v0, r_v2, r_g2, r_k_k, r_k_a, r_r_k, r_gn_w, r_gn_b, w_up_h, w_up_m, w_up_r, w_out, w_mlp_in, w_mlp_out):
    batch, seq, d = x.shape
    assert d == D_MODEL and M_HEAD_DIM == CHUNK and R_HEAD_DIM == CHUNK
    l_real = seq + CHUNK
    lp = -(-l_real // ROW_ALIGN) * ROW_ALIGN
    t = batch * lp
    tm_proj = _tile(t, (1280, 640))
    tm = _tile(t, (640,))
    tm_tail = _tile(t, (512, 256))

    h = jnp.concatenate([jnp.zeros((batch, N_PAD, d), x.dtype),
                         jnp.broadcast_to(meta.astype(x.dtype), (batch, N_META, d)),
                         x, jnp.zeros((batch, lp - l_real, d), x.dtype)], axis=1).reshape(t, d)

    v_first = None
    for l in range(DEPTH):
        vres_w = w_in_vres[l - 1] if l > 0 else jnp.zeros((d, R_MV_RANK), w_in.dtype)
        w_comb = _split_in_cols(w_in[l], vres_w).astype(BF16)
        p = _proj(h, ln1_w[l][None, :], w_comb, tm_proj, 1280)

        o_h = _hgrn(p, hg_lb_logits, hg_norm_w[l].reshape(1, H_WIDTH), l, batch, lp)
        o_m = _mamba(p, m_conv_w[l], m_conv_b[l], m_dt_bias[l], m_a_log[l], m_d[l], m_norm_w[l], batch, lp)

        mu = r_mu[l][None, :]
        o = 3 * R_WIDTH
        prm = {
            "mu_rkv": mu[:, :o],
            "mu_wa": jnp.concatenate([_pad_cols(mu[:, o:o + R_DECAY_RANK], LANES),
                                      _pad_cols(mu[:, o + R_DECAY_RANK:o + R_DECAY_RANK + R_AAA_RANK], LANES)], axis=1),
            "mu_g1": mu[:, o + R_DECAY_RANK + R_AAA_RANK:],
            "w0": r_w0[l][None, :], "w2": _pad_rows(r_w2[l], LANES).astype(BF16),
            "a0": r_a0[l][None, :], "a2": _pad_rows(r_a2[l], LANES).astype(BF16),
            "g2": r_g2[l].astype(BF16),
            "k_k": r_k_k[l][None, :], "k_a": r_k_a[l][None, :], "r_k": r_r_k[l].reshape(1, R_WIDTH),
            "gn_w": r_gn_w[l][None, :], "gn_b": r_gn_b[l][None, :],
        }
        if l == 0:
            o_r, v_first = _rwkv(p, None, prm, True, batch, lp)
        else:
            prm["mu_vr"] = _pad_cols(r_mu_vres[l - 1][None, :], LANES)
            prm["v0"] = r_v0[l - 1][None, :]
            prm["v2"] = _pad_rows(r_v2[l - 1], LANES).astype(BF16)
            (o_r,) = _rwkv(p, v_first, prm, False, batch, lp)

        mixed = _merge(o_h, o_m, o_r, p, w_up_h[l].astype(BF16), w_up_m[l].astype(BF16),
                       w_up_r[l].astype(BF16), tm, 1024)
        h = _tail(h, mixed, w_out[l].astype(BF16), ln2_w[l][None, :], w_mlp_in[l].astype(BF16),
                  w_mlp_out[l].astype(BF16), lnf_w[None, :], tm_tail, 1024, lp, l == DEPTH - 1)

    return h.reshape(batch, lp, d)[:, CHUNK:l_real]
```
